```python
import math
import jax, jax.numpy as jnp
from jax import lax
import numpy as np

D_MODEL = 1024
BATCH = 8
SEQ = 4096
DEPTH = 1

MIX_WIDTH = D_MODEL
DIFF_WIDTH = MIX_WIDTH // 2
RET_WIDTH = MIX_WIDTH - DIFF_WIDTH
N_DIFF_HEADS = 4
DIFF_V_DIM = DIFF_WIDTH // N_DIFF_HEADS
DIFF_QK_DIM = DIFF_V_DIM // 2
N_RET_HEADS = 4
RET_V_DIM = RET_WIDTH // N_RET_HEADS
RET_QK_DIM = RET_V_DIM // 2
DIFF_QK_COLS = N_DIFF_HEADS * 2 * DIFF_QK_DIM
RET_QK_COLS = N_RET_HEADS * RET_QK_DIM
SPLIT_SIZES = (DIFF_QK_COLS, DIFF_QK_COLS, DIFF_WIDTH, RET_QK_COLS, RET_QK_COLS, RET_WIDTH, RET_WIDTH)
SPLIT_POINTS = (512, 1024, 1536, 1792, 2048, 2560)
IN_COLS = 3072

ROPE_THETA = 500000.0
ROPE_FRACTION = 4
RET_THETA = 10000.0
Q_BLOCK = 128
RET_CHUNK = 128

N_KEYS = 128
N_EXPERTS = N_KEYS * N_KEYS
PEER_HEADS = 8
PEER_TOPK = 16
PEER_KEY_DIM = 256
PEER_TOKEN_BLOCK = 128

EPS = 1e-6

kernel_name = 'hybrid_diffattn_retnet_peer_encoder'


def rmsnorm(x, g):
    x32 = x.astype(jnp.float32)
    y = x32 * lax.rsqrt(jnp.mean(x32 * x32, axis=-1, keepdims=True) + EPS)
    return (y * g.astype(jnp.float32)).astype(x.dtype)


def group_norm(o, g):
    mu = jnp.mean(o, axis=-1, keepdims=True)
    c = o - mu
    var = jnp.mean(c * c, axis=-1, keepdims=True)
    return c * lax.rsqrt(var + EPS) * g.astype(jnp.float32)


def rotate_prefix(x, pos, inv_freq):
    half = inv_freq.shape[0]
    rot = 2 * half
    ang = pos[:, None] * inv_freq[None, :]
    shape = (1, pos.shape[0]) + (1,) * (x.ndim - 3) + (half,)
    cos = jnp.cos(ang).reshape(shape)
    sin = jnp.sin(ang).reshape(shape)
    xf = x.astype(jnp.float32)
    x1 = xf[..., :half]
    x2 = xf[..., half:rot]
    out = jnp.concatenate([x1 * cos - x2 * sin, x2 * cos + x1 * sin, xf[..., rot:]], axis=-1)
    return out.astype(x.dtype)


def diff_attention(q, k, v, lam):
    B, S, H, _, d = q.shape
    nb = S // Q_BLOCK
    scale = d ** -0.5
    q_blocks = q.reshape(B, nb, Q_BLOCK, H, 2, d).transpose(1, 0, 2, 3, 4, 5)

    def one_block(qb):
        s = jnp.einsum('bqhmd,bkhmd->bhmqk', qb, k).astype(jnp.float32) * scale
        p = jax.nn.softmax(s, axis=-1)
        a = p[:, :, 0] - lam * p[:, :, 1]
        return jnp.einsum('bhqk,bkhe->bqhe', a.astype(v.dtype), v)

    o = lax.map(one_block, q_blocks)
    return o.transpose(1, 0, 2, 3, 4).reshape(B, S, H, v.shape[-1])


def retention_direction(q, k, v, log_gamma, strict):
    B, S, H, dk = q.shape
    dv = v.shape[-1]
    C = RET_CHUNK
    nc = S // C
    qc = q.reshape(B, nc, C, H, dk).transpose(1, 0, 3, 2, 4)
    kc = k.reshape(B, nc, C, H, dk).transpose(1, 0, 3, 2, 4)
    vc = v.reshape(B, nc, C, H, dv).transpose(1, 0, 3, 2, 4)
    idx = jnp.arange(C, dtype=jnp.float32)
    dist = idx[:, None] - idx[None, :]
    mask = (dist > 0) if strict else (dist >= 0)
    lg = log_gamma[:, None, None]
    decay_in = jnp.where(mask[None], jnp.exp(lg * jnp.maximum(dist, 0.0)[None]), 0.0)
    q_decay = jnp.exp(log_gamma[:, None] * (idx + 1.0)[None])[None, :, :, None]
    k_decay = jnp.exp(log_gamma[:, None] * (C - 1.0 - idx)[None])[None, :, :, None]
    chunk_decay = jnp.exp(log_gamma * C)[None, :, None, None]

    def step(state, inp):
        qi, ki, vi = inp
        inner = jnp.einsum('bhid,bhjd->bhij', qi, ki) * decay_in[None]
        o = jnp.einsum('bhij,bhjv->bhiv', inner, vi) + jnp.einsum('bhid,bhdv->bhiv', qi * q_decay, state)
        state = state * chunk_decay + jnp.einsum('bhjd,bhjv->bhdv', ki * k_decay, vi)
        return state, o

    state0 = jnp.zeros((B, H, dk, dv), jnp.float32)
    _, o = lax.scan(step, state0, (qc, kc, vc))
    return o.transpose(1, 0, 3, 2, 4).reshape(B, S, H, dv)


def peer_ffn(xn, w_query, sub_keys, expert_down, expert_up):
    B, S, D = xn.shape
    T = B * S
    nb = T // PEER_TOKEN_BLOCK
    x_blocks = xn.reshape(nb, PEER_TOKEN_BLOCK, D)

    def one_block(xb):
        tb = xb.shape[0]
        q = (xb @ w_query).reshape(tb, PEER_HEADS, 2, PEER_KEY_DIM // 2)
        s = jnp.einsum('thpd,hpnd->thpn', q, sub_keys).astype(jnp.float32)
        top_s, top_i = lax.top_k(s, PEER_TOPK)
        cand_s = (top_s[:, :, 0, :, None] + top_s[:, :, 1, None, :]).reshape(tb, PEER_HEADS, PEER_TOPK * PEER_TOPK)
        cand_i = (top_i[:, :, 0, :, None] * N_KEYS + top_i[:, :, 1, None, :]).reshape(tb, PEER_HEADS, PEER_TOPK * PEER_TOPK)
        best_s, pos = lax.top_k(cand_s, PEER_TOPK)
        eid = jnp.take_along_axis(cand_i, pos, axis=-1)
        gate = jax.nn.softmax(best_s, axis=-1)
        u = jnp.take(expert_down, eid, axis=0)
        act = jax.nn.gelu(jnp.einsum('thkd,td->thk', u, xb).astype(jnp.float32), approximate=False)
        v = jnp.take(expert_up, eid, axis=0)
        return jnp.einsum('thk,thkd->td', (gate * act).astype(v.dtype), v)

    out = lax.map(one_block, x_blocks)
    return out.reshape(B, S, D).astype(xn.dtype)


def setup_inputs(seed: int = 0) -> dict:
    key = jax.random.key(seed)
    ks = jax.random.split(key, 16)
    f32 = jnp.float32
    nrm = jax.random.normal
    x = nrm(ks[0], (BATCH, SEQ, D_MODEL), f32)
    attn_norm_g = 1.0 + 0.02 * nrm(ks[1], (DEPTH, D_MODEL), f32)
    w_in = nrm(ks[2], (DEPTH, D_MODEL, IN_COLS), f32) * D_MODEL ** -0.5
    diff_lambda = 0.1 * nrm(ks[3], (DEPTH, 4, DIFF_QK_DIM), f32)
    diff_norm_g = 1.0 + 0.02 * nrm(ks[4], (DEPTH, DIFF_V_DIM), f32)
    gammas = 1.0 - jnp.exp2(-5.0 - jnp.arange(N_RET_HEADS, dtype=f32))
    base = jnp.log(-jnp.log(gammas))
    ret_log_decay = base[None, None, :] + 0.05 * nrm(ks[5], (DEPTH, 2, N_RET_HEADS), f32)
    ret_norm_g = 1.0 + 0.02 * nrm(ks[6], (DEPTH, N_RET_HEADS, RET_V_DIM), f32)
    w_out = nrm(ks[7], (DEPTH, MIX_WIDTH, D_MODEL), f32) * MIX_WIDTH ** -0.5
    ffn_norm_g = 1.0 + 0.02 * nrm(ks[8], (DEPTH, D_MODEL), f32)
    peer_w_query = nrm(ks[9], (DEPTH, D_MODEL, PEER_HEADS * PEER_KEY_DIM), f32) * D_MODEL ** -0.5
    peer_sub_keys = nrm(ks[10], (DEPTH, PEER_HEADS, 2, N_KEYS, PEER_KEY_DIM // 2), f32) * (PEER_KEY_DIM // 2) ** -0.5
    peer_u = nrm(ks[11], (DEPTH, N_EXPERTS, D_MODEL), f32) * D_MODEL ** -0.5
    peer_v = nrm(ks[12], (DEPTH, N_EXPERTS, D_MODEL), f32) * PEER_HEADS ** -0.5
    final_norm_g = 1.0 + 0.02 * nrm(ks[13], (D_MODEL,), f32)
    return {'x': x, 'attn_norm_g': attn_norm_g, 'w_in': w_in, 'diff_lambda': diff_lambda,
            'diff_norm_g': diff_norm_g, 'ret_log_decay': ret_log_decay, 'ret_norm_g': ret_norm_g,
            'w_out': w_out, 'ffn_norm_g': ffn_norm_g, 'peer_w_query': peer_w_query,
            'peer_sub_keys': peer_sub_keys, 'peer_u': peer_u, 'peer_v': peer_v,
            'final_norm_g': final_norm_g}


def reference(x, attn_norm_g, w_in, diff_lambda, diff_norm_g, ret_log_decay, ret_norm_g,
              w_out, ffn_norm_g, peer_w_query, peer_sub_keys, peer_u, peer_v, final_norm_g):
    B, S, _ = x.shape
    pos = jnp.arange(S, dtype=jnp.float32)
    rot_dim = DIFF_QK_DIM // ROPE_FRACTION
    rope_inv = jnp.power(jnp.float32(ROPE_THETA), -jnp.arange(rot_dim // 2, dtype=jnp.float32) * 2.0 / rot_dim)
    ret_inv = 1.0 / jnp.power(jnp.float32(RET_THETA), jnp.linspace(0.0, 1.0, RET_QK_DIM // 2, dtype=jnp.float32))

    for l in range(DEPTH):
        lambda_init = 0.8 - 0.6 * math.exp(-0.3 * l)
        h = rmsnorm(x, attn_norm_g[l])
        proj = h @ w_in[l]
        dq, dk, dv, rq, rk, rv, rg = jnp.split(proj, SPLIT_POINTS, axis=-1)

        dq = rotate_prefix(dq.reshape(B, S, N_DIFF_HEADS, 2, DIFF_QK_DIM), pos, rope_inv)
        dk = rotate_prefix(dk.reshape(B, S, N_DIFF_HEADS, 2, DIFF_QK_DIM), pos, rope_inv)
        dv = dv.reshape(B, S, N_DIFF_HEADS, DIFF_V_DIM)
        lam_p = diff_lambda[l].astype(jnp.float32)
        lam = jnp.exp(jnp.sum(lam_p[0] * lam_p[1])) - jnp.exp(jnp.sum(lam_p[2] * lam_p[3])) + lambda_init
        a = diff_attention(dq, dk, dv, lam)
        a = rmsnorm(a, diff_norm_g[l]).astype(jnp.float32) * (1.0 - lambda_init)
        a = a.reshape(B, S, DIFF_WIDTH)

        rq = rotate_prefix(rq.reshape(B, S, N_RET_HEADS, RET_QK_DIM), pos, ret_inv).astype(jnp.float32)
        rk = rotate_prefix(rk.reshape(B, S, N_RET_HEADS, RET_QK_DIM), pos, ret_inv).astype(jnp.float32) * RET_QK_DIM ** -0.5
        rv = rv.reshape(B, S, N_RET_HEADS, RET_V_DIM).astype(jnp.float32)
        log_gamma = -jnp.exp(ret_log_decay[l].astype(jnp.float32))
        r_fwd = retention_direction(rq, rk, rv, log_gamma[0], False)
        r_bwd = retention_direction(rq[:, ::-1], rk[:, ::-1], rv[:, ::-1], log_gamma[1], True)[:, ::-1]
        r = group_norm(r_fwd + r_bwd, ret_norm_g[l]).reshape(B, S, RET_WIDTH)
        r = jax.nn.silu(rg.astype(jnp.float32)) * r

        mix = jnp.concatenate([a, r], axis=-1).astype(x.dtype)
        x = x + mix @ w_out[l]

        x = x + peer_ffn(rmsnorm(x, ffn_norm_g[l]), peer_w_query[l], peer_sub_keys[l], peer_u[l], peer_v[l])

    return rmsnorm(x, final_norm_g)
```

```python
import functools
import math

import jax
import jax.numpy as jnp
from jax import lax
from jax.experimental import pallas as pl
from jax.experimental.pallas import tpu as pltpu

F32 = jnp.float32
BF16 = jnp.bfloat16

D_MODEL = 1024
N_DIFF_HEADS = 4
DIFF_QK_DIM = 64
N_RET_HEADS = 4
RET_QK_DIM = 64
IN_COLS = 3072
ROPE_THETA = 500000.0
ROPE_FRACTION = 4
RET_THETA = 10000.0
N_KEYS = 128
PEER_HEADS = 8
PEER_TOPK = 16
EPS = 1e-6
LAMBDA_INIT = 0.8 - 0.6 * math.exp(0.0)

LANES = 128
SUBLANES = 8
MXU_DIM = 256
VMEM_BYTES_V7X = 64 * 1024 * 1024

NT_DIMS = (((1,), (1,)), ((), ()))
TN_DIMS = (((0,), (0,)), ((), ()))


def _compiler_params(semantics, vmem_mib):
    return pltpu.CompilerParams(dimension_semantics=semantics,
                                vmem_limit_bytes=vmem_mib * 1024 * 1024)


def _rotary_tables(seq, inv_freq, half):
    pos = jnp.arange(seq, dtype=F32)
    lane = jnp.arange(LANES)
    lp = lane % 64
    rotated = lp < 2 * half
    freq = jnp.where(rotated, lp % half, 0)
    ang = pos[:, None] * inv_freq[freq][None, :]
    cos = jnp.where(rotated[None, :], jnp.cos(ang), 1.0)
    sin = jnp.sin(ang)
    sin_lo = jnp.where((lp < half)[None, :], -sin, 0.0)
    sin_hi = jnp.where((rotated & (lp >= half))[None, :], sin, 0.0)
    return jnp.stack([cos, sin_lo, sin_hi]).astype(F32)


def _rotate(p, tab_ref, half):
    up = pltpu.roll(p, LANES - half, axis=1)
    dn = pltpu.roll(p, half, axis=1)
    return p * tab_ref[0] + up * tab_ref[1] + dn * tab_ref[2]


def _inproj_kernel(x_ref, g_ref, w_ref, dtab_ref, rtab_ref, o_ref):
    x = x_ref[...]
    h = (x * lax.rsqrt(jnp.mean(x * x, axis=-1, keepdims=True) + EPS) * g_ref[...]).astype(BF16)
    chunk = 512
    for c in range(IN_COLS // chunk):
        p = jnp.dot(h, w_ref[:, c * chunk:(c + 1) * chunk], preferred_element_type=F32)
        for j in range(chunk // LANES):
            slab = c * (chunk // LANES) + j
            ps = p[:, j * LANES:(j + 1) * LANES]
            if slab < 8:
                ps = _rotate(ps, dtab_ref, DIFF_QK_DIM // ROPE_FRACTION // 2)
                if slab < 4:
                    ps = ps * (DIFF_QK_DIM ** -0.5)
            elif 12 <= slab < 16:
                ps = _rotate(ps, rtab_ref, RET_QK_DIM // 2)
                if slab >= 14:
                    ps = ps * (RET_QK_DIM ** -0.5)
            o_ref[:, slab * LANES:(slab + 1) * LANES] = ps.astype(BF16)


def _inproj(x2d, g, w_bf16, dtab, rtab, seq, tm):
    tokens = x2d.shape[0]
    nseq = seq // tm
    return pl.pallas_call(
        _inproj_kernel,
        grid=(tokens // tm,),
        in_specs=[
            pl.BlockSpec((tm, D_MODEL), lambda i: (i, 0)),
            pl.BlockSpec((1, D_MODEL), lambda i: (0, 0)),
            pl.BlockSpec((D_MODEL, IN_COLS), lambda i: (0, 0)),
            pl.BlockSpec((3, tm, LANES), lambda i: (0, i % nseq, 0)),
            pl.BlockSpec((3, tm, LANES), lambda i: (0, i % nseq, 0)),
        ],
        out_specs=pl.BlockSpec((tm, IN_COLS), lambda i: (i, 0)),
        out_shape=jax.ShapeDtypeStruct((tokens, IN_COLS), BF16),
        compiler_params=_compiler_params(("parallel",), 48),
    )(x2d, g, w_bf16, dtab, rtab)


def _diffattn_kernel(lam_ref, q_ref, k_ref, v_ref, g_ref, o_ref):
    q = q_ref[0]
    k = k_ref[0]
    v = v_ref[0]
    lane = lax.broadcasted_iota(jnp.int32, q.shape, 1)
    zero = jnp.zeros_like(q)
    q1 = jnp.where(lane < DIFF_QK_DIM, q, zero)
    q2 = jnp.where(lane >= DIFF_QK_DIM, q, zero)

    def softmax_parts(qm):
        s = lax.dot_general(qm, k, NT_DIMS, preferred_element_type=F32)
        e = jnp.exp(s - jnp.max(s, axis=-1, keepdims=True))
        return e, 1.0 / jnp.sum(e, axis=-1, keepdims=True)

    e1, r1 = softmax_parts(q1)
    e2, r2 = softmax_parts(q2)
    a = (e1 * r1 - e2 * (lam_ref[0] * r2)).astype(BF16)
    o = jnp.dot(a, v, preferred_element_type=F32)
    o = o * lax.rsqrt(jnp.mean(o * o, axis=-1, keepdims=True) + EPS) * g_ref[...]
    o_ref[0] = (o * (1.0 - LAMBDA_INIT)).astype(BF16)


def _diffattn(lam, proj3, g, tq):
    batch, seq, _ = proj3.shape
    return pl.pallas_call(
        _diffattn_kernel,
        grid=(batch, N_DIFF_HEADS, seq // tq),
        in_specs=[
            pl.BlockSpec(memory_space=pltpu.SMEM),
            pl.BlockSpec((1, tq, LANES), lambda b, h, i: (b, i, h)),
            pl.BlockSpec((1, seq, LANES), lambda b, h, i: (b, 0, 4 + h)),
            pl.BlockSpec((1, seq, LANES), lambda b, h, i: (b, 0, 8 + h)),
            pl.BlockSpec((1, LANES), lambda b, h, i: (0, 0)),
        ],
        out_specs=pl.BlockSpec((1, tq, LANES), lambda b, h, i: (b, i, h)),
        out_shape=jax.ShapeDtypeStruct((batch, seq, N_DIFF_HEADS * LANES), BF16),
        compiler_params=_compiler_params(("parallel", "parallel", "parallel"), 48),
    )(lam, proj3, proj3, proj3, g)


def _retention_kernel(lg_ref, q_ref, k_ref, v_ref, rg_ref, gn_ref, o_ref, acc_ref, *, chunk):
    hp = pl.program_id(1)
    seq = q_ref.shape[1]
    nchunks = seq // chunk
    lane = lax.broadcasted_iota(jnp.int32, (chunk, LANES), 1)
    ri = lax.broadcasted_iota(jnp.int32, (chunk, chunk), 0).astype(F32)
    ci = lax.broadcasted_iota(jnp.int32, (chunk, chunk), 1).astype(F32)
    pos = lax.broadcasted_iota(jnp.int32, (chunk, 1), 0).astype(F32)

    for hh in range(2):
        hmask = (lane >= RET_QK_DIM * hh) & (lane < RET_QK_DIM * (hh + 1))
        col = slice(hh * LANES, (hh + 1) * LANES)
        for direction in range(2):
            lg = lg_ref[direction, 2 * hp + hh]
            if direction == 0:
                dist = ri - ci
                valid = dist >= 0.0
                q_decay = jnp.exp(lg * (pos + 1.0))
                k_decay = jnp.exp(lg * (chunk - 1.0 - pos))
            else:
                dist = ci - ri
                valid = dist > 0.0
                q_decay = jnp.exp(lg * (chunk - pos))
                k_decay = jnp.exp(lg * pos)
            decay = jnp.where(valid, jnp.exp(lg * jnp.maximum(dist, 0.0)), 0.0)
            chunk_decay = jnp.exp(jnp.full((1, 1), lg * chunk, F32))

            def body(n, state, direction=direction, hmask=hmask, col=col, decay=decay,
                     q_decay=q_decay, k_decay=k_decay, chunk_decay=chunk_decay):
                c = n if direction == 0 else nchunks - 1 - n
                rows = pl.ds(pl.multiple_of(c * chunk, chunk), chunk)
                qc = q_ref[0, rows, :]
                kc = k_ref[0, rows, :]
                vc = v_ref[0, rows, col]
                qh = jnp.where(hmask, qc, jnp.zeros_like(qc))
                inner = lax.dot_general(qh, kc, NT_DIMS, preferred_element_type=F32) * decay
                o = jnp.dot(inner.astype(BF16), vc, preferred_element_type=F32)
                o = o + jnp.dot((qh.astype(F32) * q_decay).astype(BF16), state.astype(BF16),
                                preferred_element_type=F32)
                kd = (kc.astype(F32) * k_decay).astype(BF16)
                state = state * chunk_decay + lax.dot_general(kd, vc, TN_DIMS,
                                                              preferred_element_type=F32)
                if direction == 0:
                    acc_ref[rows, col] = o
                else:
                    acc_ref[rows, col] += o
                return state

            lax.fori_loop(0, nchunks, body, jnp.zeros((LANES, LANES), F32))

    for hh in range(2):
        col = slice(hh * LANES, (hh + 1) * LANES)
        o = acc_ref[:, col]
        cen = o - jnp.mean(o, axis=-1, keepdims=True)
        var = jnp.mean(cen * cen, axis=-1, keepdims=True)
        y = cen * lax.rsqrt(var + EPS) * gn_ref[:, col]
        gate = rg_ref[0, :, col].astype(F32)
        o_ref[0, :, col] = (y * gate * jax.nn.sigmoid(gate)).astype(BF16)


def _retention(log_gamma, proj3, gn, chunk):
    batch, seq, _ = proj3.shape
    wide = 2 * LANES
    return pl.pallas_call(
        functools.partial(_retention_kernel, chunk=chunk),
        grid=(batch, N_RET_HEADS // 2),
        in_specs=[
            pl.BlockSpec(memory_space=pltpu.SMEM),
            pl.BlockSpec((1, seq, LANES), lambda b, h: (b, 0, 12 + h)),
            pl.BlockSpec((1, seq, LANES), lambda b, h: (b, 0, 14 + h)),
            pl.BlockSpec((1, seq, wide), lambda b, h: (b, 0, 8 + h)),
            pl.BlockSpec((1, seq, wide), lambda b, h: (b, 0, 10 + h)),
            pl.BlockSpec((1, wide), lambda b, h: (0, h)),
        ],
        out_specs=pl.BlockSpec((1, seq, wide), lambda b, h: (b, 0, h)),
        out_shape=jax.ShapeDtypeStruct((batch, seq, N_RET_HEADS * LANES), BF16),
        scratch_shapes=[pltpu.VMEM((seq, wide), F32)],
        compiler_params=_compiler_params(("parallel", "parallel"), 48),
    )(log_gamma, proj3, proj3, proj3, proj3, gn)


def _outproj_kernel(a_ref, r_ref, x_ref, wo_ref, fg_ref, wq_ref, keys_ref, x1_ref, xn_ref, st_ref):
    half = wo_ref.shape[0] // 2
    y = x_ref[...]
    y = y + jnp.dot(a_ref[...], wo_ref[0:half, :], preferred_element_type=F32)
    y = y + jnp.dot(r_ref[...], wo_ref[half:2 * half, :], preferred_element_type=F32)
    x1_ref[...] = y
    xn = (y * lax.rsqrt(jnp.mean(y * y, axis=-1, keepdims=True) + EPS) * fg_ref[...]).astype(BF16)
    xn_ref[...] = xn
    q = jnp.dot(xn, wq_ref[...], preferred_element_type=F32).astype(BF16)
    for hp in range(2 * PEER_HEADS):
        st_ref[hp] = lax.dot_general(keys_ref[hp], q[:, hp * LANES:(hp + 1) * LANES], NT_DIMS,
                                     preferred_element_type=F32)


def _outproj(a2d, r2d, x2d, wo, fg, wq, keys, tm):
    tokens = x2d.shape[0]
    nsets = 2 * PEER_HEADS
    return pl.pallas_call(
        _outproj_kernel,
        grid=(tokens // tm,),
        in_specs=[
            pl.BlockSpec((tm, D_MODEL // 2), lambda i: (i, 0)),
            pl.BlockSpec((tm, D_MODEL // 2), lambda i: (i, 0)),
            pl.BlockSpec((tm, D_MODEL), lambda i: (i, 0)),
            pl.BlockSpec((D_MODEL, D_MODEL), lambda i: (0, 0)),
            pl.BlockSpec((1, D_MODEL), lambda i: (0, 0)),
            pl.BlockSpec((D_MODEL, nsets * LANES), lambda i: (0, 0)),
            pl.BlockSpec((nsets, N_KEYS, LANES), lambda i: (0, 0, 0)),
        ],
        out_specs=[
            pl.BlockSpec((tm, D_MODEL), lambda i: (i, 0)),
            pl.BlockSpec((tm, D_MODEL), lambda i: (i, 0)),
            pl.BlockSpec((nsets, N_KEYS, tm), lambda i: (0, 0, i)),
        ],
        out_shape=[
            jax.ShapeDtypeStruct((tokens, D_MODEL), F32),
            jax.ShapeDtypeStruct((tokens, D_MODEL), BF16),
            jax.ShapeDtypeStruct((nsets, N_KEYS, tokens), F32),
        ],
        compiler_params=_compiler_params(("parallel",), 48),
    )(a2d, r2d, x2d, wo, fg, wq, keys)


def _extract_top(s, code, big, count):
    vals, codes = [], []
    for _ in range(count):
        m = jnp.max(s, axis=0, keepdims=True)
        sel = jnp.min(jnp.where(s == m, code, big), axis=0, keepdims=True)
        vals.append(m)
        codes.append(sel)
        s = jnp.where(code == sel, -jnp.inf, s)
    return jnp.concatenate(vals, axis=0), jnp.concatenate(codes, axis=0)


def _pick_rows(table, sel):
    out = jnp.zeros_like(sel)
    for a in range(PEER_TOPK):
        out = jnp.where(sel == float(a), table[a:a + 1, :], out)
    return out


def _topk_kernel(st_ref, i_ref, j_ref, g_ref, i_scr, j_scr, g_scr):
    tb = st_ref.shape[2]
    k = PEER_TOPK
    key_code = lax.broadcasted_iota(jnp.int32, (N_KEYS, tb), 0).astype(F32)
    row8 = lax.broadcasted_iota(jnp.int32, (SUBLANES, tb), 0).astype(F32)
    neg = jnp.full((SUBLANES, tb), -jnp.inf, F32)

    def head_body(h, carry):
        v1, i1 = _extract_top(st_ref[2 * h], key_code, float(N_KEYS), k)
        v2, i2 = _extract_top(st_ref[2 * h + 1], key_code, float(N_KEYS), k)
        cands, codes = [], []
        for a in range(k // 2):
            nb = k // (a + 1)
            for b0 in range(0, nb, SUBLANES):
                c = v1[a:a + 1, :] + v2[b0:b0 + SUBLANES, :]
                if nb - b0 < SUBLANES:
                    c = jnp.where(row8 < float(nb - b0), c, neg)
                cands.append(c)
                codes.append(row8 + float(a * k + b0))
        cands.append(v1[k // 2:k, :] + v2[0:1, :])
        codes.append((row8 + float(k // 2)) * float(k))
        best, code = _extract_top(jnp.concatenate(cands, axis=0), jnp.concatenate(codes, axis=0),
                                  float(k * k), k)
        a_sel = jnp.floor(code * (1.0 / k))
        b_sel = code - a_sel * k
        e = jnp.exp(best - best[0:1, :])
        gate = e / jnp.sum(e, axis=0, keepdims=True)
        rows = pl.ds(pl.multiple_of(h * k, k), k)
        i_scr[rows, :] = _pick_rows(i1, a_sel)
        j_scr[rows, :] = _pick_rows(i2, b_sel)
        g_scr[rows, :] = gate
        return carry

    lax.fori_loop(0, PEER_HEADS, head_body, 0)
    i_ref[...] = i_scr[...].T
    j_ref[...] = j_scr[...].T
    g_ref[...] = g_scr[...].T


def _topk(st, tb):
    nsets, _, tokens = st.shape
    slots = PEER_HEADS * PEER_TOPK
    spec = pl.BlockSpec((tb, slots), lambda i: (i, 0))
    shape = jax.ShapeDtypeStruct((tokens, slots), F32)
    return pl.pallas_call(
        _topk_kernel,
        grid=(tokens // tb,),
        in_specs=[pl.BlockSpec((nsets, N_KEYS, tb), lambda i: (0, 0, i))],
        out_specs=[spec, spec, spec],
        out_shape=[shape, shape, shape],
        scratch_shapes=[pltpu.VMEM((slots, tb), F32)] * 3,
        compiler_params=_compiler_params(("parallel",), 32),
    )(st)


def _peer_kernel(xn_ref, i_ref, j_ref, g_ref, u_ref, v_ref, x1_ref, fg_ref, o_ref, w_scr, acc_ref,
                 *, row_stride):
    e = pl.program_id(1)
    bt = xn_ref.shape[0]

    @pl.when(e == 0)
    def _build_weights():
        acc_ref[...] = jnp.zeros_like(acc_ref)
        sub = lax.broadcasted_iota(jnp.int32, (N_KEYS, LANES), 0).astype(F32)

        def token_body(t, carry):
            irow = i_ref[pl.ds(t, 1), :]
            jrow = j_ref[pl.ds(t, 1), :]
            grow = g_ref[pl.ds(t, 1), :]
            pt = jnp.where(sub == irow, 1.0, 0.0).astype(BF16)
            qt = jnp.where(sub == jrow, grow, 0.0).astype(BF16)
            w = lax.dot_general(pt, qt, NT_DIMS, preferred_element_type=F32)
            w_scr[pl.ds(t, N_KEYS, stride=row_stride), :] = w
            return carry

        lax.fori_loop(0, bt, token_body, 0)

    i0 = 2 * e
    w = jnp.concatenate(
        [w_scr[pl.ds(pl.multiple_of(i0 * row_stride, SUBLANES), bt), :],
         w_scr[pl.ds(pl.multiple_of((i0 + 1) * row_stride, SUBLANES), bt), :]], axis=1)
    act = lax.dot_general(xn_ref[...], u_ref[...], NT_DIMS, preferred_element_type=F32)
    gelu = 0.5 * act * (1.0 + lax.erf(act * (2.0 ** -0.5)))
    acc_ref[...] += jnp.dot((w * gelu).astype(BF16), v_ref[...], preferred_element_type=F32)

    @pl.when(e == pl.num_programs(1) - 1)
    def _finish():
        y = x1_ref[...] + acc_ref[...]
        o_ref[...] = y * lax.rsqrt(jnp.mean(y * y, axis=-1, keepdims=True) + EPS) * fg_ref[...]


def _peer(xn, i_sel, j_sel, gates, u_bf16, v_bf16, x1, fg, bt):
    tokens = xn.shape[0]
    n_experts = u_bf16.shape[0]
    row_stride = bt + SUBLANES
    slots = PEER_HEADS * PEER_TOPK
    tok = lambda width: pl.BlockSpec((bt, width), lambda t, e: (t, 0))
    return pl.pallas_call(
        functools.partial(_peer_kernel, row_stride=row_stride),
        grid=(tokens // bt, n_experts // MXU_DIM),
        in_specs=[
            tok(D_MODEL), tok(slots), tok(slots), tok(slots),
            pl.BlockSpec((MXU_DIM, D_MODEL), lambda t, e: (e, 0)),
            pl.BlockSpec((MXU_DIM, D_MODEL), lambda t, e: (e, 0)),
            tok(D_MODEL),
            pl.BlockSpec((1, D_MODEL), lambda t, e: (0, 0)),
        ],
        out_specs=tok(D_MODEL),
        out_shape=jax.ShapeDtypeStruct((tokens, D_MODEL), F32),
        scratch_shapes=[pltpu.VMEM((N_KEYS * row_stride, LANES), F32),
                        pltpu.VMEM((bt, D_MODEL), F32)],
        compiler_params=_compiler_params(("parallel", "arbitrary"), 48),
    )(xn, i_sel, j_sel, gates, u_bf16, v_bf16, x1, fg)


def kernel(x, attn_norm_g, w_in, diff_lambda, diff_norm_g, ret_log_decay, ret_norm_g, w_out,
           ffn_norm_g, peer_w_query, peer_sub_keys, peer_u, peer_v, final_norm_g):
    batch, seq, d_model = x.shape
    assert d_model == D_MODEL and seq % LANES == 0
    tokens = batch * seq
    tm = min(512, seq)
    x2d = x.reshape(tokens, D_MODEL)

    rot_dim = DIFF_QK_DIM // ROPE_FRACTION
    rope_inv = jnp.power(jnp.float32(ROPE_THETA), -jnp.arange(rot_dim // 2, dtype=F32) * 2.0 / rot_dim)
    ret_inv = 1.0 / jnp.power(jnp.float32(RET_THETA), jnp.linspace(0.0, 1.0, RET_QK_DIM // 2, dtype=F32))
    dtab = _rotary_tables(seq, rope_inv, rot_dim // 2)
    rtab = _rotary_tables(seq, ret_inv, RET_QK_DIM // 2)

    proj = _inproj(x2d, attn_norm_g[0].reshape(1, D_MODEL), w_in[0].astype(BF16), dtab, rtab, seq, tm)
    proj3 = proj.reshape(batch, seq, IN_COLS)

    lam_p = diff_lambda[0].astype(F32)
    lam = (jnp.exp(jnp.sum(lam_p[0] * lam_p[1])) - jnp.exp(jnp.sum(lam_p[2] * lam_p[3]))
           + LAMBDA_INIT).reshape(1).astype(F32)
    attn = _diffattn(lam, proj3, diff_norm_g[0].reshape(1, LANES), min(256, seq))

    log_gamma = -jnp.exp(ret_log_decay[0].astype(F32))
    ret = _retention(log_gamma, proj3, ret_norm_g[0].reshape(1, N_RET_HEADS * LANES), min(256, seq))

    x1, xn, scores_t = _outproj(
        attn.reshape(tokens, D_MODEL // 2), ret.reshape(tokens, D_MODEL // 2), x2d,
        w_out[0].astype(BF16), ffn_norm_g[0].reshape(1, D_MODEL), peer_w_query[0].astype(BF16),
        peer_sub_keys[0].reshape(2 * PEER_HEADS, N_KEYS, LANES).astype(BF16), tm)

    i_sel, j_sel, gates = _topk(scores_t, LANES)

    y = _peer(xn, i_sel, j_sel, gates, peer_u[0].astype(BF16), peer_v[0].astype(BF16), x1,
              final_norm_g.reshape(1, D_MODEL), min(256, tokens))
    return y.reshape(batch, seq, D_MODEL)
```

```python
import functools
import math

import jax
import jax.numpy as jnp
from jax import lax
from jax.experimental import pallas as pl
from jax.experimental.pallas import tpu as pltpu

F32 = jnp.float32
BF16 = jnp.bfloat16

D_MODEL = 1024
N_DIFF_HEADS = 4
DIFF_QK_DIM = 64
N_RET_HEADS = 4
RET_QK_DIM = 64
IN_COLS = 3072
ROPE_THETA = 500000.0
ROPE_FRACTION = 4
RET_THETA = 10000.0
N_KEYS = 128
PEER_HEADS = 8
PEER_TOPK = 16
EPS = 1e-6
LAMBDA_INIT = 0.8 - 0.6 * math.exp(0.0)

LANES = 128
SUBLANES = 8
MXU_DIM = 256
VMEM_BYTES_V7X = 64 * 1024 * 1024

NT_DIMS = (((1,), (1,)), ((), ()))
TN_DIMS = (((0,), (0,)), ((), ()))


def _compiler_params(semantics, vmem_mib):
    return pltpu.CompilerParams(dimension_semantics=semantics,
                                vmem_limit_bytes=vmem_mib * 1024 * 1024)


def _rotary_tables(seq, inv_freq, half):
    pos = jnp.arange(seq, dtype=F32)
    lane = jnp.arange(LANES)
    lp = lane % 64
    rotated = lp < 2 * half
    freq = jnp.where(rotated, lp % half, 0)
    ang = pos[:, None] * inv_freq[freq][None, :]
    cos = jnp.where(rotated[None, :], jnp.cos(ang), 1.0)
    sin = jnp.sin(ang)
    sin_lo = jnp.where((lp < half)[None, :], -sin, 0.0)
    sin_hi = jnp.where((rotated & (lp >= half))[None, :], sin, 0.0)
    return jnp.stack([cos, sin_lo, sin_hi]).astype(F32)


def _rotate(p, tab_ref, half):
    up = pltpu.roll(p, LANES - half, axis=1)
    dn = pltpu.roll(p, half, axis=1)
    return p * tab_ref[0] + up * tab_ref[1] + dn * tab_ref[2]


def _inproj_kernel(x_ref, g_ref, w_ref, dtab_ref, rtab_ref, o_ref):
    x = x_ref[...]
    h = (x * lax.rsqrt(jnp.mean(x * x, axis=-1, keepdims=True) + EPS) * g_ref[...]).astype(BF16)
    chunk = 512
    for c in range(IN_COLS // chunk):
        p = jnp.dot(h, w_ref[:, c * chunk:(c + 1) * chunk], preferred_element_type=F32)
        for j in range(chunk // LANES):
            slab = c * (chunk // LANES) + j
            ps = p[:, j * LANES:(j + 1) * LANES]
            if slab < 8:
                ps = _rotate(ps, dtab_ref, DIFF_QK_DIM // ROPE_FRACTION // 2)
                if slab < 4:
                    ps = ps * (DIFF_QK_DIM ** -0.5 * math.log2(math.e))
            elif 12 <= slab < 16:
                ps = _rotate(ps, rtab_ref, RET_QK_DIM // 2)
                if slab >= 14:
                    ps = ps * (RET_QK_DIM ** -0.5)
            o_ref[:, slab * LANES:(slab + 1) * LANES] = ps.astype(BF16)


def _inproj(x2d, g, w_bf16, dtab, rtab, seq, tm):
    tokens = x2d.shape[0]
    nseq = seq // tm
    return pl.pallas_call(
        _inproj_kernel,
        grid=(tokens // tm,),
        in_specs=[
            pl.BlockSpec((tm, D_MODEL), lambda i: (i, 0)),
            pl.BlockSpec((1, D_MODEL), lambda i: (0, 0)),
            pl.BlockSpec((D_MODEL, IN_COLS), lambda i: (0, 0)),
            pl.BlockSpec((3, tm, LANES), lambda i: (0, i % nseq, 0)),
            pl.BlockSpec((3, tm, LANES), lambda i: (0, i % nseq, 0)),
        ],
        out_specs=pl.BlockSpec((tm, IN_COLS), lambda i: (i, 0)),
        out_shape=jax.ShapeDtypeStruct((tokens, IN_COLS), BF16),
        compiler_params=_compiler_params(("parallel",), 48),
    )(x2d, g, w_bf16, dtab, rtab)


def _diffattn_kernel(lam_ref, q_ref, k_ref, v_ref, g_ref, o_ref):
    q = q_ref[0]
    k = k_ref[0]
    v = v_ref[0]
    lane = lax.broadcasted_iota(jnp.int32, q.shape, 1)
    zero = jnp.zeros_like(q)
    q1 = jnp.where(lane < DIFF_QK_DIM, q, zero)
    q2 = jnp.where(lane >= DIFF_QK_DIM, q, zero)

    v_ones = jnp.concatenate([v, jnp.ones_like(v)], axis=1)
    seq = k.shape[0]
    kv_chunk = min(seq, 512)

    def attend(qm):
        m = jnp.full((qm.shape[0], 1), -jnp.inf, F32)
        acc = jnp.zeros((qm.shape[0], 2 * LANES), F32)
        for c in range(seq // kv_chunk):
            rows = slice(c * kv_chunk, (c + 1) * kv_chunk)
            s = lax.dot_general(qm, k[rows], NT_DIMS, preferred_element_type=F32)
            m_new = jnp.maximum(m, jnp.max(s, axis=-1, keepdims=True))
            p = jnp.exp2(s - m_new).astype(BF16)
            acc = acc * jnp.exp2(m - m_new) + jnp.dot(p, v_ones[rows], preferred_element_type=F32)
            m = m_new
        return acc[:, :LANES] / acc[:, LANES:]

    o = attend(q1) - lam_ref[0] * attend(q2)
    o = o * lax.rsqrt(jnp.mean(o * o, axis=-1, keepdims=True) + EPS) * g_ref[...]
    o_ref[0] = (o * (1.0 - LAMBDA_INIT)).astype(BF16)


def _diffattn(lam, proj3, g, tq):
    batch, seq, _ = proj3.shape
    return pl.pallas_call(
        _diffattn_kernel,
        grid=(batch, N_DIFF_HEADS, seq // tq),
        in_specs=[
            pl.BlockSpec(memory_space=pltpu.SMEM),
            pl.BlockSpec((1, tq, LANES), lambda b, h, i: (b, i, h)),
            pl.BlockSpec((1, seq, LANES), lambda b, h, i: (b, 0, 4 + h)),
            pl.BlockSpec((1, seq, LANES), lambda b, h, i: (b, 0, 8 + h)),
            pl.BlockSpec((1, LANES), lambda b, h, i: (0, 0)),
        ],
        out_specs=pl.BlockSpec((1, tq, LANES), lambda b, h, i: (b, i, h)),
        out_shape=jax.ShapeDtypeStruct((batch, seq, N_DIFF_HEADS * LANES), BF16),
        compiler_params=_compiler_params(("parallel", "parallel", "parallel"), 48),
    )(lam, proj3, proj3, proj3, g)


def _retention_kernel(lg_ref, q_ref, k_ref, v_ref, rg_ref, gn_ref, o_ref, acc_ref, *, chunk):
    hp = pl.program_id(1)
    seq = q_ref.shape[1]
    nchunks = seq // chunk
    lane = lax.broadcasted_iota(jnp.int32, (chunk, LANES), 1)
    ri = lax.broadcasted_iota(jnp.int32, (chunk, chunk), 0).astype(F32)
    ci = lax.broadcasted_iota(jnp.int32, (chunk, chunk), 1).astype(F32)
    pos = lax.broadcasted_iota(jnp.int32, (chunk, 1), 0).astype(F32)

    for hh in range(2):
        hmask = (lane >= RET_QK_DIM * hh) & (lane < RET_QK_DIM * (hh + 1))
        col = slice(hh * LANES, (hh + 1) * LANES)
        for direction in range(2):
            lg = lg_ref[direction, 2 * hp + hh]
            if direction == 0:
                dist = ri - ci
                valid = dist >= 0.0
                q_decay = jnp.exp(lg * (pos + 1.0))
                k_decay = jnp.exp(lg * (chunk - 1.0 - pos))
            else:
                dist = ci - ri
                valid = dist > 0.0
                q_decay = jnp.exp(lg * (chunk - pos))
                k_decay = jnp.exp(lg * pos)
            decay = jnp.where(valid, jnp.exp(lg * jnp.maximum(dist, 0.0)), 0.0)
            chunk_decay = jnp.exp(jnp.full((1, 1), lg * chunk, F32))

            def body(n, state, direction=direction, hmask=hmask, col=col, decay=decay,
                     q_decay=q_decay, k_decay=k_decay, chunk_decay=chunk_decay):
                c = n if direction == 0 else nchunks - 1 - n
                rows = pl.ds(pl.multiple_of(c * chunk, chunk), chunk)
                qc = q_ref[0, rows, :]
                kc = k_ref[0, rows, :]
                vc = v_ref[0, rows, col]
                qh = jnp.where(hmask, qc, jnp.zeros_like(qc))
                inner = lax.dot_general(qh, kc, NT_DIMS, preferred_element_type=F32) * decay
                o = jnp.dot(inner.astype(BF16), vc, preferred_element_type=F32)
                o = o + jnp.dot((qh.astype(F32) * q_decay).astype(BF16), state.astype(BF16),
                                preferred_element_type=F32)
                kd = (kc.astype(F32) * k_decay).astype(BF16)
                state = state * chunk_decay + lax.dot_general(kd, vc, TN_DIMS,
                                                              preferred_element_type=F32)
                if direction == 0:
                    acc_ref[rows, col] = o
                else:
                    acc_ref[rows, col] += o
                return state

            lax.fori_loop(0, nchunks, body, jnp.zeros((LANES, LANES), F32))

    for hh in range(2):
        col = slice(hh * LANES, (hh + 1) * LANES)
        o = acc_ref[:, col]
        cen = o - jnp.mean(o, axis=-1, keepdims=True)
        var = jnp.mean(cen * cen, axis=-1, keepdims=True)
        y = cen * lax.rsqrt(var + EPS) * gn_ref[:, col]
        gate = rg_ref[0, :, col].astype(F32)
        o_ref[0, :, col] = (y * gate * jax.nn.sigmoid(gate)).astype(BF16)


def _retention(log_gamma, proj3, gn, chunk):
    batch, seq, _ = proj3.shape
    wide = 2 * LANES
    return pl.pallas_call(
        functools.partial(_retention_kernel, chunk=chunk),
        grid=(batch, N_RET_HEADS // 2),
        in_specs=[
            pl.BlockSpec(memory_space=pltpu.SMEM),
            pl.BlockSpec((1, seq, LANES), lambda b, h: (b, 0, 12 + h)),
            pl.BlockSpec((1, seq, LANES), lambda b, h: (b, 0, 14 + h)),
            pl.BlockSpec((1, seq, wide), lambda b, h: (b, 0, 8 + h)),
            pl.BlockSpec((1, seq, wide), lambda b, h: (b, 0, 10 + h)),
            pl.BlockSpec((1, wide), lambda b, h: (0, h)),
        ],
        out_specs=pl.BlockSpec((1, seq, wide), lambda b, h: (b, 0, h)),
        out_shape=jax.ShapeDtypeStruct((batch, seq, N_RET_HEADS * LANES), BF16),
        scratch_shapes=[pltpu.VMEM((seq, wide), F32)],
        compiler_params=_compiler_params(("parallel", "parallel"), 48),
    )(log_gamma, proj3, proj3, proj3, proj3, gn)


def _outproj_kernel(a_ref, r_ref, x_ref, wo_ref, fg_ref, wq_ref, keys_ref, x1_ref, xn_ref, st_ref):
    half = wo_ref.shape[0] // 2
    y = x_ref[...]
    y = y + jnp.dot(a_ref[...], wo_ref[0:half, :], preferred_element_type=F32)
    y = y + jnp.dot(r_ref[...], wo_ref[half:2 * half, :], preferred_element_type=F32)
    x1_ref[...] = y
    xn = (y * lax.rsqrt(jnp.mean(y * y, axis=-1, keepdims=True) + EPS) * fg_ref[...]).astype(BF16)
    xn_ref[...] = xn
    q = jnp.dot(xn, wq_ref[...], preferred_element_type=F32).astype(BF16)
    for hp in range(2 * PEER_HEADS):
        st_ref[hp] = lax.dot_general(keys_ref[hp], q[:, hp * LANES:(hp + 1) * LANES], NT_DIMS,
                                     preferred_element_type=F32)


def _outproj(a2d, r2d, x2d, wo, fg, wq, keys, tm):
    tokens = x2d.shape[0]
    nsets = 2 * PEER_HEADS
    return pl.pallas_call(
        _outproj_kernel,
        grid=(tokens // tm,),
        in_specs=[
            pl.BlockSpec((tm, D_MODEL // 2), lambda i: (i, 0)),
            pl.BlockSpec((tm, D_MODEL // 2), lambda i: (i, 0)),
            pl.BlockSpec((tm, D_MODEL), lambda i: (i, 0)),
            pl.BlockSpec((D_MODEL, D_MODEL), lambda i: (0, 0)),
            pl.BlockSpec((1, D_MODEL), lambda i: (0, 0)),
            pl.BlockSpec((D_MODEL, nsets * LANES), lambda i: (0, 0)),
            pl.BlockSpec((nsets, N_KEYS, LANES), lambda i: (0, 0, 0)),
        ],
        out_specs=[
            pl.BlockSpec((tm, D_MODEL), lambda i: (i, 0)),
            pl.BlockSpec((tm, D_MODEL), lambda i: (i, 0)),
            pl.BlockSpec((nsets, N_KEYS, tm), lambda i: (0, 0, i)),
        ],
        out_shape=[
            jax.ShapeDtypeStruct((tokens, D_MODEL), F32),
            jax.ShapeDtypeStruct((tokens, D_MODEL), BF16),
            jax.ShapeDtypeStruct((nsets, N_KEYS, tokens), F32),
        ],
        compiler_params=_compiler_params(("parallel",), 48),
    )(a2d, r2d, x2d, wo, fg, wq, keys)


def _extract_top(s, code, big, count):
    vals, codes = [], []
    for _ in range(count):
        m = jnp.max(s, axis=0, keepdims=True)
        sel = jnp.min(jnp.where(s == m, code, big), axis=0, keepdims=True)
        vals.append(m)
        codes.append(sel)
        s = jnp.where(code == sel, -jnp.inf, s)
    return jnp.concatenate(vals, axis=0), jnp.concatenate(codes, axis=0)


def _pick_rows(table, sel):
    out = jnp.zeros_like(sel)
    for a in range(PEER_TOPK):
        out = jnp.where(sel == float(a), table[a:a + 1, :], out)
    return out


def _topk_kernel(st_ref, i_ref, j_ref, g_ref, i_scr, j_scr, g_scr):
    tb = st_ref.shape[2]
    k = PEER_TOPK
    key_code = lax.broadcasted_iota(jnp.int32, (N_KEYS, tb), 0).astype(F32)
    row8 = lax.broadcasted_iota(jnp.int32, (SUBLANES, tb), 0).astype(F32)
    neg = jnp.full((SUBLANES, tb), -jnp.inf, F32)

    def head_body(h, carry):
        v1, i1 = _extract_top(st_ref[2 * h], key_code, float(N_KEYS), k)
        v2, i2 = _extract_top(st_ref[2 * h + 1], key_code, float(N_KEYS), k)
        cands, codes = [], []
        for a in range(k // 2):
            nb = k // (a + 1)
            for b0 in range(0, nb, SUBLANES):
                c = v1[a:a + 1, :] + v2[b0:b0 + SUBLANES, :]
                if nb - b0 < SUBLANES:
                    c = jnp.where(row8 < float(nb - b0), c, neg)
                cands.append(c)
                codes.append(row8 + float(a * k + b0))
        cands.append(v1[k // 2:k, :] + v2[0:1, :])
        codes.append((row8 + float(k // 2)) * float(k))
        best, code = _extract_top(jnp.concatenate(cands, axis=0), jnp.concatenate(codes, axis=0),
                                  float(k * k), k)
        a_sel = jnp.floor(code * (1.0 / k))
        b_sel = code - a_sel * k
        e = jnp.exp(best - best[0:1, :])
        gate = e / jnp.sum(e, axis=0, keepdims=True)
        rows = pl.ds(pl.multiple_of(h * k, k), k)
        i_scr[rows, :] = _pick_rows(i1, a_sel)
        j_scr[rows, :] = _pick_rows(i2, b_sel)
        g_scr[rows, :] = gate
        return carry

    lax.fori_loop(0, PEER_HEADS, head_body, 0)
    i_ref[...] = i_scr[...].T
    j_ref[...] = j_scr[...].T
    g_ref[...] = g_scr[...].T


def _topk(st, tb):
    nsets, _, tokens = st.shape
    slots = PEER_HEADS * PEER_TOPK
    spec = pl.BlockSpec((tb, slots), lambda i: (i, 0))
    shape = jax.ShapeDtypeStruct((tokens, slots), F32)
    return pl.pallas_call(
        _topk_kernel,
        grid=(tokens // tb,),
        in_specs=[pl.BlockSpec((nsets, N_KEYS, tb), lambda i: (0, 0, i))],
        out_specs=[spec, spec, spec],
        out_shape=[shape, shape, shape],
        scratch_shapes=[pltpu.VMEM((slots, tb), F32)] * 3,
        compiler_params=_compiler_params(("parallel",), 32),
    )(st)


def _peer_kernel(xn_ref, i_ref, j_ref, g_ref, u_ref, v_ref, x1_ref, fg_ref, o_ref,
                 w_scr, gated_scr, *, row_stride, n_tiles):
    e = pl.program_id(1)
    bt = xn_ref.shape[0]
    tile = u_ref.shape[0]

    @pl.when(e == 0)
    def _build_weights():
        o_ref[...] = jnp.zeros_like(o_ref)
        gated_scr[...] = jnp.zeros_like(gated_scr)
        sub = lax.broadcasted_iota(jnp.int32, (N_KEYS, LANES), 0).astype(F32)
        zeros = jnp.zeros((N_KEYS, LANES), BF16)

        def onehots(t):
            irow = i_ref[pl.ds(t, 1), :]
            jrow = j_ref[pl.ds(t, 1), :]
            grow = g_ref[pl.ds(t, 1), :] * 0.5
            pt = jnp.where(sub == irow, 1.0, 0.0).astype(BF16)
            qt = jnp.where(sub == jrow, grow, 0.0).astype(BF16)
            return pt, qt

        def pair_body(p, carry):
            pt_a, qt_a = onehots(p)
            pt_b, qt_b = onehots(p + bt // 2)
            lhs = jnp.concatenate([pt_a, pt_b], axis=1)
            rhs = jnp.concatenate([jnp.concatenate([qt_a, zeros], axis=1),
                                   jnp.concatenate([zeros, qt_b], axis=1)], axis=0)
            w2 = lax.dot_general(lhs, rhs, NT_DIMS, preferred_element_type=F32)
            w_scr[pl.ds(p, N_KEYS, stride=row_stride), :] = w2[:, :LANES]
            w_scr[pl.ds(p + bt // 2, N_KEYS, stride=row_stride), :] = w2[:, LANES:]
            return carry

        lax.fori_loop(0, bt // 2, pair_body, 0, unroll=8)

    o_ref[...] += jnp.dot(gated_scr[...], v_ref[...], preferred_element_type=F32)

    act = lax.dot_general(xn_ref[...], u_ref[...], NT_DIMS, preferred_element_type=F32)
    i0 = jnp.minimum(e, n_tiles - 1) * (tile // N_KEYS)
    w = jnp.concatenate(
        [w_scr[pl.ds(pl.multiple_of((i0 + ii) * row_stride, SUBLANES), bt), :]
         for ii in range(tile // N_KEYS)], axis=1)
    gated = w * (act * (1.0 + lax.erf(act * (2.0 ** -0.5))))
    gated_scr[...] = gated.astype(BF16)

    @pl.when(e == n_tiles)
    def _finish():
        y = x1_ref[...] + o_ref[...]
        o_ref[...] = y * lax.rsqrt(jnp.mean(y * y, axis=-1, keepdims=True) + EPS) * fg_ref[...]


def _peer(xn, i_sel, j_sel, gates, u_bf16, v_bf16, x1, fg, bt, tile):
    tokens = xn.shape[0]
    n_tiles = u_bf16.shape[0] // tile
    row_stride = bt + SUBLANES
    slots = PEER_HEADS * PEER_TOPK
    tok = lambda width: pl.BlockSpec((bt, width), lambda t, e: (t, 0))
    return pl.pallas_call(
        functools.partial(_peer_kernel, row_stride=row_stride, n_tiles=n_tiles),
        grid=(tokens // bt, n_tiles + 1),
        in_specs=[
            tok(D_MODEL), tok(slots), tok(slots), tok(slots),
            pl.BlockSpec((tile, D_MODEL), lambda t, e: (jnp.minimum(e, n_tiles - 1), 0)),
            pl.BlockSpec((tile, D_MODEL), lambda t, e: (jnp.maximum(e - 1, 0), 0)),
            tok(D_MODEL),
            pl.BlockSpec((1, D_MODEL), lambda t, e: (0, 0)),
        ],
        out_specs=tok(D_MODEL),
        out_shape=jax.ShapeDtypeStruct((tokens, D_MODEL), F32),
        scratch_shapes=[pltpu.VMEM((N_KEYS * row_stride, LANES), F32),
                        pltpu.VMEM((bt, tile), BF16)],
        compiler_params=_compiler_params(("parallel", "arbitrary"), 60),
    )(xn, i_sel, j_sel, gates, u_bf16, v_bf16, x1, fg)


def kernel(x, attn_norm_g, w_in, diff_lambda, diff_norm_g, ret_log_decay, ret_norm_g, w_out,
           ffn_norm_g, peer_w_query, peer_sub_keys, peer_u, peer_v, final_norm_g):
    batch, seq, d_model = x.shape
    assert d_model == D_MODEL and seq % LANES == 0
    tokens = batch * seq
    tm = min(512, seq)
    x2d = x.reshape(tokens, D_MODEL)

    rot_dim = DIFF_QK_DIM // ROPE_FRACTION
    rope_inv = jnp.power(jnp.float32(ROPE_THETA), -jnp.arange(rot_dim // 2, dtype=F32) * 2.0 / rot_dim)
    ret_inv = 1.0 / jnp.power(jnp.float32(RET_THETA), jnp.linspace(0.0, 1.0, RET_QK_DIM // 2, dtype=F32))
    dtab = _rotary_tables(seq, rope_inv, rot_dim // 2)
    rtab = _rotary_tables(seq, ret_inv, RET_QK_DIM // 2)

    proj = _inproj(x2d, attn_norm_g[0].reshape(1, D_MODEL), w_in[0].astype(BF16), dtab, rtab, seq, tm)
    proj3 = proj.reshape(batch, seq, IN_COLS)

    lam_p = diff_lambda[0].astype(F32)
    lam = (jnp.exp(jnp.sum(lam_p[0] * lam_p[1])) - jnp.exp(jnp.sum(lam_p[2] * lam_p[3]))
           + LAMBDA_INIT).reshape(1).astype(F32)
    attn = _diffattn(lam, proj3, diff_norm_g[0].reshape(1, LANES), min(256, seq))

    log_gamma = -jnp.exp(ret_log_decay[0].astype(F32))
    ret = _retention(log_gamma, proj3, ret_norm_g[0].reshape(1, N_RET_HEADS * LANES), min(256, seq))

    x1, xn, scores_t = _outproj(
        attn.reshape(tokens, D_MODEL // 2), ret.reshape(tokens, D_MODEL // 2), x2d,
        w_out[0].astype(BF16), ffn_norm_g[0].reshape(1, D_MODEL), peer_w_query[0].astype(BF16),
        peer_sub_keys[0].reshape(2 * PEER_HEADS, N_KEYS, LANES).astype(BF16), tm)

    i_sel, j_sel, gates = _topk(scores_t, LANES)

    y = _peer(xn, i_sel, j_sel, gates, peer_u[0].astype(BF16), peer_v[0].astype(BF16), x1,
              final_norm_g.reshape(1, D_MODEL), min(512, tokens), 512)
    return y.reshape(batch, seq, D_MODEL)
```

```python
import functools
import math

import jax
import jax.numpy as jnp
from jax import lax
from jax.experimental import pallas as pl
from jax.experimental.pallas import tpu as pltpu

F32 = jnp.float32
BF16 = jnp.bfloat16

D_MODEL = 1024
N_DIFF_HEADS = 4
DIFF_QK_DIM = 64
N_RET_HEADS = 4
RET_QK_DIM = 64
IN_COLS = 3072
ROPE_THETA = 500000.0
ROPE_FRACTION = 4
RET_THETA = 10000.0
N_KEYS = 128
PEER_HEADS = 8
PEER_TOPK = 16
EPS = 1e-6
LAMBDA_INIT = 0.8 - 0.6 * math.exp(0.0)

LANES = 128
SUBLANES = 8
MXU_DIM = 256
VMEM_BYTES_V7X = 64 * 1024 * 1024

NT_DIMS = (((1,), (1,)), ((), ()))
TN_DIMS = (((0,), (0,)), ((), ()))


def _compiler_params(semantics, vmem_mib):
    return pltpu.CompilerParams(dimension_semantics=semantics,
                                vmem_limit_bytes=vmem_mib * 1024 * 1024)


def _rotary_tables(seq, inv_freq, half):
    pos = jnp.arange(seq, dtype=F32)
    lane = jnp.arange(LANES)
    lp = lane % 64
    rotated = lp < 2 * half
    freq = jnp.where(rotated, lp % half, 0)
    ang = pos[:, None] * inv_freq[freq][None, :]
    cos = jnp.where(rotated[None, :], jnp.cos(ang), 1.0)
    sin = jnp.sin(ang)
    sin_lo = jnp.where((lp < half)[None, :], -sin, 0.0)
    sin_hi = jnp.where((rotated & (lp >= half))[None, :], sin, 0.0)
    return jnp.stack([cos, sin_lo, sin_hi]).astype(F32)


def _rotate(p, tab_ref, half):
    up = pltpu.roll(p, LANES - half, axis=1)
    dn = pltpu.roll(p, half, axis=1)
    return p * tab_ref[0] + up * tab_ref[1] + dn * tab_ref[2]


def _inproj_kernel(x_ref, g_ref, w_ref, dtab_ref, rtab_ref, o_ref):
    x = x_ref[...]
    h = (x * lax.rsqrt(jnp.mean(x * x, axis=-1, keepdims=True) + EPS) * g_ref[...]).astype(BF16)
    chunk = 512
    for c in range(IN_COLS // chunk):
        p = jnp.dot(h, w_ref[:, c * chunk:(c + 1) * chunk], preferred_element_type=F32)
        for j in range(chunk // LANES):
            slab = c * (chunk // LANES) + j
            ps = p[:, j * LANES:(j + 1) * LANES]
            if slab < 8:
                ps = _rotate(ps, dtab_ref, DIFF_QK_DIM // ROPE_FRACTION // 2)
                if slab < 4:
                    ps = ps * (DIFF_QK_DIM ** -0.5 * math.log2(math.e))
            elif 12 <= slab < 16:
                ps = _rotate(ps, rtab_ref, RET_QK_DIM // 2)
                if slab >= 14:
                    ps = ps * (RET_QK_DIM ** -0.5)
            o_ref[:, slab * LANES:(slab + 1) * LANES] = ps.astype(BF16)


def _inproj(x2d, g, w_bf16, dtab, rtab, seq, tm):
    tokens = x2d.shape[0]
    nseq = seq // tm
    return pl.pallas_call(
        _inproj_kernel,
        grid=(tokens // tm,),
        in_specs=[
            pl.BlockSpec((tm, D_MODEL), lambda i: (i, 0)),
            pl.BlockSpec((1, D_MODEL), lambda i: (0, 0)),
            pl.BlockSpec((D_MODEL, IN_COLS), lambda i: (0, 0)),
            pl.BlockSpec((3, tm, LANES), lambda i: (0, i % nseq, 0)),
            pl.BlockSpec((3, tm, LANES), lambda i: (0, i % nseq, 0)),
        ],
        out_specs=pl.BlockSpec((tm, IN_COLS), lambda i: (i, 0)),
        out_shape=jax.ShapeDtypeStruct((tokens, IN_COLS), BF16),
        compiler_params=_compiler_params(("parallel",), 48),
    )(x2d, g, w_bf16, dtab, rtab)


def _diffattn_kernel(lam_ref, q_ref, k_ref, v_ref, g_ref, o_ref):
    q = q_ref[0]
    k = k_ref[0]
    v = v_ref[0]
    lane = lax.broadcasted_iota(jnp.int32, q.shape, 1)
    zero = jnp.zeros_like(q)
    q1 = jnp.where(lane < DIFF_QK_DIM, q, zero)
    q2 = jnp.where(lane >= DIFF_QK_DIM, q, zero)

    v_ones = jnp.concatenate([v, jnp.ones_like(v)], axis=1)
    seq = k.shape[0]
    kv_chunk = min(seq, 512)

    def attend(qm):
        m = jnp.full((qm.shape[0], 1), -jnp.inf, F32)
        acc = jnp.zeros((qm.shape[0], 2 * LANES), F32)
        for c in range(seq // kv_chunk):
            rows = slice(c * kv_chunk, (c + 1) * kv_chunk)
            s = lax.dot_general(qm, k[rows], NT_DIMS, preferred_element_type=F32)
            m_new = jnp.maximum(m, jnp.max(s, axis=-1, keepdims=True))
            p = jnp.exp2(s - m_new).astype(BF16)
            acc = acc * jnp.exp2(m - m_new) + jnp.dot(p, v_ones[rows], preferred_element_type=F32)
            m = m_new
        return acc[:, :LANES] / acc[:, LANES:]

    o = attend(q1) - lam_ref[0] * attend(q2)
    o = o * lax.rsqrt(jnp.mean(o * o, axis=-1, keepdims=True) + EPS) * g_ref[...]
    o_ref[0] = (o * (1.0 - LAMBDA_INIT)).astype(BF16)


def _diffattn(lam, proj3, g, tq):
    batch, seq, _ = proj3.shape
    return pl.pallas_call(
        _diffattn_kernel,
        grid=(batch, N_DIFF_HEADS, seq // tq),
        in_specs=[
            pl.BlockSpec(memory_space=pltpu.SMEM),
            pl.BlockSpec((1, tq, LANES), lambda b, h, i: (b, i, h)),
            pl.BlockSpec((1, seq, LANES), lambda b, h, i: (b, 0, 4 + h)),
            pl.BlockSpec((1, seq, LANES), lambda b, h, i: (b, 0, 8 + h)),
            pl.BlockSpec((1, LANES), lambda b, h, i: (0, 0)),
        ],
        out_specs=pl.BlockSpec((1, tq, LANES), lambda b, h, i: (b, i, h)),
        out_shape=jax.ShapeDtypeStruct((batch, seq, N_DIFF_HEADS * LANES), BF16),
        compiler_params=_compiler_params(("parallel", "parallel", "parallel"), 48),
    )(lam, proj3, proj3, proj3, g)


def _retention_kernel(lg_ref, q_ref, k_ref, v_ref, rg_ref, gn_ref, o_ref, acc_ref, *, chunk):
    hp = pl.program_id(1)
    seq = q_ref.shape[1]
    nchunks = seq // chunk
    lane = lax.broadcasted_iota(jnp.int32, (chunk, LANES), 1)
    ri = lax.broadcasted_iota(jnp.int32, (chunk, chunk), 0).astype(F32)
    ci = lax.broadcasted_iota(jnp.int32, (chunk, chunk), 1).astype(F32)
    pos = lax.broadcasted_iota(jnp.int32, (chunk, 1), 0).astype(F32)

    streams = []
    for hh in range(2):
        hmask = (lane >= RET_QK_DIM * hh) & (lane < RET_QK_DIM * (hh + 1))
        col = slice(hh * LANES, (hh + 1) * LANES)
        for direction in range(2):
            lg = lg_ref[direction, 2 * hp + hh]
            if direction == 0:
                dist = ri - ci
                valid = dist >= 0.0
                q_decay = jnp.exp(lg * (pos + 1.0))
                k_decay = jnp.exp(lg * (chunk - 1.0 - pos))
            else:
                dist = ci - ri
                valid = dist > 0.0
                q_decay = jnp.exp(lg * (chunk - pos))
                k_decay = jnp.exp(lg * pos)
            decay = jnp.where(valid, jnp.exp(lg * jnp.maximum(dist, 0.0)), 0.0)
            chunk_decay = jnp.exp(jnp.full((1, 1), lg * chunk, F32))
            streams.append((direction, hmask, col, decay, q_decay, k_decay, chunk_decay))

    acc_ref[...] = jnp.zeros_like(acc_ref)

    def body(n, states):
        new_states = []
        for state, (direction, hmask, col, decay, q_decay, k_decay, chunk_decay) in zip(states, streams):
            c = n if direction == 0 else nchunks - 1 - n
            rows = pl.ds(pl.multiple_of(c * chunk, chunk), chunk)
            qc = q_ref[0, rows, :]
            kc = k_ref[0, rows, :]
            vc = v_ref[0, rows, col]
            qh = jnp.where(hmask, qc, jnp.zeros_like(qc))
            inner = lax.dot_general(qh, kc, NT_DIMS, preferred_element_type=F32) * decay
            o = jnp.dot(inner.astype(BF16), vc, preferred_element_type=F32)
            o = o + jnp.dot((qh.astype(F32) * q_decay).astype(BF16), state.astype(BF16),
                            preferred_element_type=F32)
            kd = (kc.astype(F32) * k_decay).astype(BF16)
            new_states.append(state * chunk_decay
                              + lax.dot_general(kd, vc, TN_DIMS, preferred_element_type=F32))
            acc_ref[rows, col] += o
        return tuple(new_states)

    lax.fori_loop(0, nchunks, body, tuple(jnp.zeros((LANES, LANES), F32) for _ in streams))

    for hh in range(2):
        col = slice(hh * LANES, (hh + 1) * LANES)
        o = acc_ref[:, col]
        cen = o - jnp.mean(o, axis=-1, keepdims=True)
        var = jnp.mean(cen * cen, axis=-1, keepdims=True)
        y = cen * lax.rsqrt(var + EPS) * gn_ref[:, col]
        gate = rg_ref[0, :, col].astype(F32)
        o_ref[0, :, col] = (y * gate * jax.nn.sigmoid(gate)).astype(BF16)


def _retention(log_gamma, proj3, gn, chunk):
    batch, seq, _ = proj3.shape
    wide = 2 * LANES
    return pl.pallas_call(
        functools.partial(_retention_kernel, chunk=chunk),
        grid=(batch, N_RET_HEADS // 2),
        in_specs=[
            pl.BlockSpec(memory_space=pltpu.SMEM),
            pl.BlockSpec((1, seq, LANES), lambda b, h: (b, 0, 12 + h)),
            pl.BlockSpec((1, seq, LANES), lambda b, h: (b, 0, 14 + h)),
            pl.BlockSpec((1, seq, wide), lambda b, h: (b, 0, 8 + h)),
            pl.BlockSpec((1, seq, wide), lambda b, h: (b, 0, 10 + h)),
            pl.BlockSpec((1, wide), lambda b, h: (0, h)),
        ],
        out_specs=pl.BlockSpec((1, seq, wide), lambda b, h: (b, 0, h)),
        out_shape=jax.ShapeDtypeStruct((batch, seq, N_RET_HEADS * LANES), BF16),
        scratch_shapes=[pltpu.VMEM((seq, wide), F32)],
        compiler_params=_compiler_params(("parallel", "parallel"), 48),
    )(log_gamma, proj3, proj3, proj3, proj3, gn)


def _outproj_kernel(a_ref, r_ref, x_ref, wo_ref, fg_ref, wq_ref, keys_ref, x1_ref, xn_ref, st_ref):
    half = wo_ref.shape[0] // 2
    y = x_ref[...]
    y = y + jnp.dot(a_ref[...], wo_ref[0:half, :], preferred_element_type=F32)
    y = y + jnp.dot(r_ref[...], wo_ref[half:2 * half, :], preferred_element_type=F32)
    x1_ref[...] = y
    xn = (y * lax.rsqrt(jnp.mean(y * y, axis=-1, keepdims=True) + EPS) * fg_ref[...]).astype(BF16)
    xn_ref[...] = xn
    q = jnp.dot(xn, wq_ref[...], preferred_element_type=F32).astype(BF16)
    for hp in range(2 * PEER_HEADS):
        st_ref[hp] = lax.dot_general(keys_ref[hp], q[:, hp * LANES:(hp + 1) * LANES], NT_DIMS,
                                     preferred_element_type=F32)


def _outproj(a2d, r2d, x2d, wo, fg, wq, keys, tm):
    tokens = x2d.shape[0]
    nsets = 2 * PEER_HEADS
    return pl.pallas_call(
        _outproj_kernel,
        grid=(tokens // tm,),
        in_specs=[
            pl.BlockSpec((tm, D_MODEL // 2), lambda i: (i, 0)),
            pl.BlockSpec((tm, D_MODEL // 2), lambda i: (i, 0)),
            pl.BlockSpec((tm, D_MODEL), lambda i: (i, 0)),
            pl.BlockSpec((D_MODEL, D_MODEL), lambda i: (0, 0)),
            pl.BlockSpec((1, D_MODEL), lambda i: (0, 0)),
            pl.BlockSpec((D_MODEL, nsets * LANES), lambda i: (0, 0)),
            pl.BlockSpec((nsets, N_KEYS, LANES), lambda i: (0, 0, 0)),
        ],
        out_specs=[
            pl.BlockSpec((tm, D_MODEL), lambda i: (i, 0)),
            pl.BlockSpec((tm, D_MODEL), lambda i: (i, 0)),
            pl.BlockSpec((nsets, N_KEYS, tm), lambda i: (0, 0, i)),
        ],
        out_shape=[
            jax.ShapeDtypeStruct((tokens, D_MODEL), F32),
            jax.ShapeDtypeStruct((tokens, D_MODEL), BF16),
            jax.ShapeDtypeStruct((nsets, N_KEYS, tokens), F32),
        ],
        compiler_params=_compiler_params(("parallel",), 48),
    )(a2d, r2d, x2d, wo, fg, wq, keys)


def _extract_top(s, code, big, count):
    vals, codes = [], []
    for _ in range(count):
        m = jnp.max(s, axis=0, keepdims=True)
        sel = jnp.min(jnp.where(s == m, code, big), axis=0, keepdims=True)
        vals.append(m)
        codes.append(sel)
        s = jnp.where(code == sel, -jnp.inf, s)
    return jnp.concatenate(vals, axis=0), jnp.concatenate(codes, axis=0)


def _ordered(a, b):
    (va, ca), (vb, cb) = a, b
    a_first = (va > vb) | ((va == vb) & (ca < cb))
    return ((jnp.where(a_first, va, vb), jnp.where(a_first, ca, cb)),
            (jnp.where(a_first, vb, va), jnp.where(a_first, cb, ca)))


def _extract_top_keys(s, row8, count):
    depth, n_stacks = 4, N_KEYS // SUBLANES // 4
    big = float(N_KEYS)
    levels = [[None] * n_stacks for _ in range(depth)]
    for st in range(n_stacks):
        e = [(s[(st * depth + d) * SUBLANES:(st * depth + d + 1) * SUBLANES, :],
              row8 + float((st * depth + d) * SUBLANES)) for d in range(depth)]
        e[0], e[1] = _ordered(e[0], e[1])
        e[2], e[3] = _ordered(e[2], e[3])
        e[0], e[2] = _ordered(e[0], e[2])
        e[1], e[3] = _ordered(e[1], e[3])
        e[1], e[2] = _ordered(e[1], e[2])
        for d in range(depth):
            levels[d][st] = e[d]
    vals, codes = [], []
    for _ in range(count):
        tops = levels[0]
        best = jnp.maximum(jnp.maximum(tops[0][0], tops[1][0]), jnp.maximum(tops[2][0], tops[3][0]))
        m = jnp.max(best, axis=0, keepdims=True)
        tied = [jnp.where(v == m, c, big) for v, c in tops]
        sel = jnp.min(jnp.minimum(jnp.minimum(tied[0], tied[1]), jnp.minimum(tied[2], tied[3])),
                      axis=0, keepdims=True)
        vals.append(m)
        codes.append(sel)
        for st in range(n_stacks):
            hit = levels[0][st][1] == sel
            for d in range(depth - 1):
                (v_up, c_up), (v_dn, c_dn) = levels[d][st], levels[d + 1][st]
                levels[d][st] = (jnp.where(hit, v_dn, v_up), jnp.where(hit, c_dn, c_up))
            v_last, c_last = levels[depth - 1][st]
            levels[depth - 1][st] = (jnp.where(hit, -jnp.inf, v_last), jnp.where(hit, big, c_last))
    return jnp.concatenate(vals, axis=0), jnp.concatenate(codes, axis=0)


def _pick_rows(table, sel):
    out = jnp.zeros_like(sel)
    for a in range(PEER_TOPK):
        out = jnp.where(sel == float(a), table[a:a + 1, :], out)
    return out


def _topk_kernel(st_ref, i_ref, j_ref, g_ref, i_scr, j_scr, g_scr):
    tb = st_ref.shape[2]
    k = PEER_TOPK
    row8 =lax.broadcasted_iota(jnp.int32, (SUBLANES, tb), 0).astype(F32)
    neg = jnp.full((SUBLANES, tb), -jnp.inf, F32)

    def head_body(h, carry):
        v1, i1 = _extract_top_keys(st_ref[2 * h], row8, k)
        v2, i2 = _extract_top_keys(st_ref[2 * h + 1], row8, k)
        cands, codes = [], []
        for a in range(k // 2):
            nb = k // (a + 1)
            for b0 in range(0, nb, SUBLANES):
                c = v1[a:a + 1, :] + v2[b0:b0 + SUBLANES, :]
                if nb - b0 < SUBLANES:
                    c = jnp.where(row8 < float(nb - b0), c, neg)
                cands.append(c)
                codes.append(row8 + float(a * k + b0))
        cands.append(v1[k // 2:k, :] + v2[0:1, :])
        codes.append((row8 + float(k // 2)) * float(k))
        best, code = _extract_top(jnp.concatenate(cands, axis=0), jnp.concatenate(codes, axis=0),
                                  float(k * k), k)
        a_sel = jnp.floor(code * (1.0 / k))
        b_sel = code - a_sel * k
        e = jnp.exp(best - best[0:1, :])
        gate = e / jnp.sum(e, axis=0, keepdims=True)
        rows = pl.ds(pl.multiple_of(h * k, k), k)
        i_scr[rows, :] = _pick_rows(i1, a_sel)
        j_scr[rows, :] = _pick_rows(i2, b_sel)
        g_scr[rows, :] = gate
        return carry

    lax.fori_loop(0, PEER_HEADS, head_body, 0, unroll=4)
    i_ref[...] = i_scr[...].T
    j_ref[...] = j_scr[...].T
    g_ref[...] = g_scr[...].T


def _topk(st, tb):
    nsets, _, tokens = st.shape
    slots = PEER_HEADS * PEER_TOPK
    spec = pl.BlockSpec((tb, slots), lambda i: (i, 0))
    shape = jax.ShapeDtypeStruct((tokens, slots), F32)
    return pl.pallas_call(
        _topk_kernel,
        grid=(tokens // tb,),
        in_specs=[pl.BlockSpec((nsets, N_KEYS, tb), lambda i: (0, 0, i))],
        out_specs=[spec, spec, spec],
        out_shape=[shape, shape, shape],
        scratch_shapes=[pltpu.VMEM((slots, tb), F32)] * 3,
        compiler_params=_compiler_params(("parallel",), 32),
    )(st)


def _peer_kernel(xn_ref, i_ref, j_ref, g_ref, u_ref, v_ref, x1_ref, fg_ref, o_ref,
                 w_scr, gated_scr, *, row_stride, n_tiles):
    e = pl.program_id(1)
    bt = xn_ref.shape[0]
    tile = u_ref.shape[0]

    @pl.when(e == 0)
    def _build_weights():
        o_ref[...] = jnp.zeros_like(o_ref)
        gated_scr[...] = jnp.zeros_like(gated_scr)
        sub = lax.broadcasted_iota(jnp.int32, (N_KEYS, LANES), 0).astype(F32)
        zeros = jnp.zeros((N_KEYS, LANES), BF16)

        def onehots(t):
            irow = i_ref[pl.ds(t, 1), :]
            jrow = j_ref[pl.ds(t, 1), :]
            grow = g_ref[pl.ds(t, 1), :] * 0.5
            pt = jnp.where(sub == irow, 1.0, 0.0).astype(BF16)
            qt = jnp.where(sub == jrow, grow, 0.0).astype(BF16)
            return pt, qt

        def pair_body(p, carry):
            pt_a, qt_a = onehots(p)
            pt_b, qt_b = onehots(p + bt // 2)
            lhs = jnp.concatenate([pt_a, pt_b], axis=1)
            rhs = jnp.concatenate([jnp.concatenate([qt_a, zeros], axis=1),
                                   jnp.concatenate([zeros, qt_b], axis=1)], axis=0)
            w2 = lax.dot_general(lhs, rhs, NT_DIMS, preferred_element_type=F32)
            w_scr[pl.ds(p, N_KEYS, stride=row_stride), :] = w2[:, :LANES]
            w_scr[pl.ds(p + bt // 2, N_KEYS, stride=row_stride), :] = w2[:, LANES:]
            return carry

        lax.fori_loop(0, bt // 2, pair_body, 0, unroll=16)

    o_ref[...] += jnp.dot(gated_scr[...], v_ref[...], preferred_element_type=F32)

    act = lax.dot_general(xn_ref[...], u_ref[...], NT_DIMS, preferred_element_type=F32)
    i0 = jnp.minimum(e, n_tiles - 1) * (tile // N_KEYS)
    w = jnp.concatenate(
        [w_scr[pl.ds(pl.multiple_of((i0 + ii) * row_stride, SUBLANES), bt), :]
         for ii in range(tile // N_KEYS)], axis=1)
    gated = w * (act * (1.0 + lax.erf(act * (2.0 ** -0.5))))
    gated_scr[...] = gated.astype(BF16)

    @pl.when(e == n_tiles)
    def _finish():
        y = x1_ref[...] + o_ref[...]
        o_ref[...] = y * lax.rsqrt(jnp.mean(y * y, axis=-1, keepdims=True) + EPS) * fg_ref[...]


def _peer(xn, i_sel, j_sel, gates, u_bf16, v_bf16, x1, fg, bt, tile):
    tokens = xn.shape[0]
    n_tiles = u_bf16.shape[0] // tile
    row_stride = bt + SUBLANES
    slots = PEER_HEADS * PEER_TOPK
    tok = lambda width: pl.BlockSpec((bt, width), lambda t, e: (t, 0))
    return pl.pallas_call(
        functools.partial(_peer_kernel, row_stride=row_stride, n_tiles=n_tiles),
        grid=(tokens // bt, n_tiles + 1),
        in_specs=[
            tok(D_MODEL), tok(slots), tok(slots), tok(slots),
            pl.BlockSpec((tile, D_MODEL), lambda t, e: (jnp.minimum(e, n_tiles - 1), 0)),
            pl.BlockSpec((tile, D_MODEL), lambda t, e: (jnp.maximum(e - 1, 0), 0)),
            tok(D_MODEL),
            pl.BlockSpec((1, D_MODEL), lambda t, e: (0, 0)),
        ],
        out_specs=tok(D_MODEL),
        out_shape=jax.ShapeDtypeStruct((tokens, D_MODEL), F32),
        scratch_shapes=[pltpu.VMEM((N_KEYS * row_stride, LANES), F32),
                        pltpu.VMEM((bt, tile), BF16)],
        compiler_params=_compiler_params(("parallel", "arbitrary"), 60),
    )(xn, i_sel, j_sel, gates, u_bf16, v_bf16, x1, fg)


def kernel(x, attn_norm_g, w_in, diff_lambda, diff_norm_g, ret_log_decay, ret_norm_g, w_out,
           ffn_norm_g, peer_w_query, peer_sub_keys, peer_u, peer_v, final_norm_g):
    batch, seq, d_model = x.shape
    assert d_model == D_MODEL and seq % LANES == 0
    tokens = batch * seq
    tm = min(512, seq)
    x2d = x.reshape(tokens, D_MODEL)

    rot_dim = DIFF_QK_DIM // ROPE_FRACTION
    rope_inv = jnp.power(jnp.float32(ROPE_THETA), -jnp.arange(rot_dim // 2, dtype=F32) * 2.0 / rot_dim)
    ret_inv = 1.0 / jnp.power(jnp.float32(RET_THETA), jnp.linspace(0.0, 1.0, RET_QK_DIM // 2, dtype=F32))
    dtab = _rotary_tables(seq, rope_inv, rot_dim // 2)
    rtab = _rotary_tables(seq, ret_inv, RET_QK_DIM // 2)

    proj = _inproj(x2d, attn_norm_g[0].reshape(1, D_MODEL), w_in[0].astype(BF16), dtab, rtab, seq, tm)
    proj3 = proj.reshape(batch, seq, IN_COLS)

    lam_p = diff_lambda[0].astype(F32)
    lam = (jnp.exp(jnp.sum(lam_p[0] * lam_p[1])) - jnp.exp(jnp.sum(lam_p[2] * lam_p[3]))
           + LAMBDA_INIT).reshape(1).astype(F32)
    attn = _diffattn(lam, proj3, diff_norm_g[0].reshape(1, LANES), min(512, seq))

    log_gamma = -jnp.exp(ret_log_decay[0].astype(F32))
    ret = _retention(log_gamma, proj3, ret_norm_g[0].reshape(1, N_RET_HEADS * LANES), min(256, seq))

    x1, xn, scores_t = _outproj(
        attn.reshape(tokens, D_MODEL // 2), ret.reshape(tokens, D_MODEL // 2), x2d,
        w_out[0].astype(BF16), ffn_norm_g[0].reshape(1, D_MODEL), peer_w_query[0].astype(BF16),
        peer_sub_keys[0].reshape(2 * PEER_HEADS, N_KEYS, LANES).astype(BF16), tm)

    i_sel, j_sel, gates = _topk(scores_t, LANES)

    y = _peer(xn, i_sel, j_sel, gates, peer_u[0].astype(BF16), peer_v[0].astype(BF16), x1,
              final_norm_g.reshape(1, D_MODEL), min(512, tokens), 1024)
    return y.reshape(batch, seq, D_MODEL)
```

```python
import functools
import math

import jax
import jax.numpy as jnp
from jax import lax
from jax.experimental import pallas as pl
from jax.experimental.pallas import tpu as pltpu

F32 = jnp.float32
BF16 = jnp.bfloat16

D_MODEL = 1024
N_DIFF_HEADS = 4
DIFF_QK_DIM = 64
N_RET_HEADS = 4
RET_QK_DIM = 64
IN_COLS = 3072
ROPE_THETA = 500000.0
ROPE_FRACTION = 4
RET_THETA = 10000.0
N_KEYS = 128
PEER_HEADS = 8
PEER_TOPK = 16
EPS = 1e-6
LAMBDA_INIT = 0.8 - 0.6 * math.exp(0.0)

LANES = 128
SUBLANES = 8
MXU_DIM = 256
VMEM_BYTES_V7X = 64 * 1024 * 1024

NT_DIMS = (((1,), (1,)), ((), ()))
TN_DIMS = (((0,), (0,)), ((), ()))


def _compiler_params(semantics, vmem_mib):
    return pltpu.CompilerParams(dimension_semantics=semantics,
                                vmem_limit_bytes=vmem_mib * 1024 * 1024)


def _rotary_tables(seq, inv_freq, half):
    pos = jnp.arange(seq, dtype=F32)
    lane = jnp.arange(LANES)
    lp = lane % 64
    rotated = lp < 2 * half
    freq = jnp.where(rotated, lp % half, 0)
    ang = pos[:, None] * inv_freq[freq][None, :]
    cos = jnp.where(rotated[None, :], jnp.cos(ang), 1.0)
    sin = jnp.sin(ang)
    sin_lo = jnp.where((lp < half)[None, :], -sin, 0.0)
    sin_hi = jnp.where((rotated & (lp >= half))[None, :], sin, 0.0)
    return jnp.stack([cos, sin_lo, sin_hi]).astype(F32)


def _rotate(p, tab_ref, half):
    up = pltpu.roll(p, LANES - half, axis=1)
    dn = pltpu.roll(p, half, axis=1)
    return p * tab_ref[0] + up * tab_ref[1] + dn * tab_ref[2]


def _inproj_kernel(x_ref, g_ref, w_ref, dtab_ref, rtab_ref, o_ref):
    x = x_ref[...]
    h = (x * lax.rsqrt(jnp.mean(x * x, axis=-1, keepdims=True) + EPS) * g_ref[...]).astype(BF16)
    chunk = 512
    for c in range(IN_COLS // chunk):
        p = jnp.dot(h, w_ref[:, c * chunk:(c + 1) * chunk], preferred_element_type=F32)
        for j in range(chunk // LANES):
            slab = c * (chunk // LANES) + j
            ps = p[:, j * LANES:(j + 1) * LANES]
            if slab < 8:
                ps = _rotate(ps, dtab_ref, DIFF_QK_DIM // ROPE_FRACTION // 2)
                if slab < 4:
                    ps = ps * (DIFF_QK_DIM ** -0.5 * math.log2(math.e))
            elif 12 <= slab < 16:
                ps = _rotate(ps, rtab_ref, RET_QK_DIM // 2)
                if slab >= 14:
                    ps = ps * (RET_QK_DIM ** -0.5)
            o_ref[:, slab * LANES:(slab + 1) * LANES] = ps.astype(BF16)


def _inproj(x2d, g, w_bf16, dtab, rtab, seq, tm):
    tokens = x2d.shape[0]
    nseq = seq // tm
    return pl.pallas_call(
        _inproj_kernel,
        grid=(tokens // tm,),
        in_specs=[
            pl.BlockSpec((tm, D_MODEL), lambda i: (i, 0)),
            pl.BlockSpec((1, D_MODEL), lambda i: (0, 0)),
            pl.BlockSpec((D_MODEL, IN_COLS), lambda i: (0, 0)),
            pl.BlockSpec((3, tm, LANES), lambda i: (0, i % nseq, 0)),
            pl.BlockSpec((3, tm, LANES), lambda i: (0, i % nseq, 0)),
        ],
        out_specs=pl.BlockSpec((tm, IN_COLS), lambda i: (i, 0)),
        out_shape=jax.ShapeDtypeStruct((tokens, IN_COLS), BF16),
        compiler_params=_compiler_params(("parallel",), 48),
    )(x2d, g, w_bf16, dtab, rtab)


def _diffattn_kernel(lam_ref, q_ref, k_ref, v_ref, g_ref, o_ref):
    q = q_ref[0]
    k = k_ref[0]
    v = v_ref[0]
    lane = lax.broadcasted_iota(jnp.int32, q.shape, 1)
    zero = jnp.zeros_like(q)
    q1 = jnp.where(lane < DIFF_QK_DIM, q, zero)
    q2 = jnp.where(lane >= DIFF_QK_DIM, q, zero)

    v_ones = jnp.concatenate([v, jnp.ones_like(v)], axis=1)
    seq = k.shape[0]
    kv_chunk = min(seq, 256)

    queries = (q1, q2)
    ms = [jnp.full((q.shape[0], 1), -jnp.inf, F32) for _ in queries]
    accs = [jnp.zeros((q.shape[0], 2 * LANES), F32) for _ in queries]
    for c in range(seq // kv_chunk):
        rows = slice(c * kv_chunk, (c + 1) * kv_chunk)
        for i, qm in enumerate(queries):
            s = lax.dot_general(qm, k[rows], NT_DIMS, preferred_element_type=F32)
            m_new = jnp.maximum(ms[i], jnp.max(s, axis=-1, keepdims=True))
            p = jnp.exp2(s - m_new).astype(BF16)
            accs[i] = (accs[i] * jnp.exp2(ms[i] - m_new)
                       + jnp.dot(p, v_ones[rows], preferred_element_type=F32))
            ms[i] = m_new
    outs = [acc[:, :LANES] / acc[:, LANES:] for acc in accs]

    o = outs[0] - lam_ref[0] * outs[1]
    o = o * lax.rsqrt(jnp.mean(o * o, axis=-1, keepdims=True) + EPS) * g_ref[...]
    o_ref[0] = (o * (1.0 - LAMBDA_INIT)).astype(BF16)


def _diffattn(lam, proj3, g, tq):
    batch, seq, _ = proj3.shape
    return pl.pallas_call(
        _diffattn_kernel,
        grid=(batch, N_DIFF_HEADS, seq // tq),
        in_specs=[
            pl.BlockSpec(memory_space=pltpu.SMEM),
            pl.BlockSpec((1, tq, LANES), lambda b, h, i: (b, i, h)),
            pl.BlockSpec((1, seq, LANES), lambda b, h, i: (b, 0, 4 + h)),
            pl.BlockSpec((1, seq, LANES), lambda b, h, i: (b, 0, 8 + h)),
            pl.BlockSpec((1, LANES), lambda b, h, i: (0, 0)),
        ],
        out_specs=pl.BlockSpec((1, tq, LANES), lambda b, h, i: (b, i, h)),
        out_shape=jax.ShapeDtypeStruct((batch, seq, N_DIFF_HEADS * LANES), BF16),
        compiler_params=_compiler_params(("parallel", "parallel", "parallel"), 48),
    )(lam, proj3, proj3, proj3, g)


def _retention_kernel(lg_ref, q_ref, k_ref, v_ref, rg_ref, gn_ref, o_ref, acc_ref, *, chunk):
    hp = pl.program_id(1)
    seq = q_ref.shape[1]
    nchunks = seq // chunk
    lane = lax.broadcasted_iota(jnp.int32, (chunk, LANES), 1)
    ri = lax.broadcasted_iota(jnp.int32, (chunk, chunk), 0).astype(F32)
    ci = lax.broadcasted_iota(jnp.int32, (chunk, chunk), 1).astype(F32)
    pos = lax.broadcasted_iota(jnp.int32, (chunk, 1), 0).astype(F32)

    streams = []
    for hh in range(2):
        hmask = (lane >= RET_QK_DIM * hh) & (lane < RET_QK_DIM * (hh + 1))
        col = slice(hh * LANES, (hh + 1) * LANES)
        for direction in range(2):
            lg = lg_ref[direction, 2 * hp + hh]
            if direction == 0:
                dist = ri - ci
                valid = dist >= 0.0
                q_decay = jnp.exp(lg * (pos + 1.0))
                k_decay = jnp.exp(lg * (chunk - 1.0 - pos))
            else:
                dist = ci - ri
                valid = dist > 0.0
                q_decay = jnp.exp(lg * (chunk - pos))
                k_decay = jnp.exp(lg * pos)
            decay = jnp.where(valid, jnp.exp(lg * jnp.maximum(dist, 0.0)), 0.0)
            chunk_decay = jnp.exp(jnp.full((1, 1), lg * chunk, F32))
            streams.append((direction, hmask, col, decay, q_decay, k_decay, chunk_decay))

    acc_ref[...] = jnp.zeros_like(acc_ref)

    def body(n, states):
        new_states = []
        for state, (direction, hmask, col, decay, q_decay, k_decay, chunk_decay) in zip(states, streams):
            c = n if direction == 0 else nchunks - 1 - n
            rows = pl.ds(pl.multiple_of(c * chunk, chunk), chunk)
            qc = q_ref[0, rows, :]
            kc = k_ref[0, rows, :]
            vc = v_ref[0, rows, col]
            qh = jnp.where(hmask, qc, jnp.zeros_like(qc))
            inner = lax.dot_general(qh, kc, NT_DIMS, preferred_element_type=F32) * decay
            o = jnp.dot(inner.astype(BF16), vc, preferred_element_type=F32)
            o = o + jnp.dot((qh.astype(F32) * q_decay).astype(BF16), state.astype(BF16),
                            preferred_element_type=F32)
            kd = (kc.astype(F32) * k_decay).astype(BF16)
            new_states.append(state * chunk_decay
                              + lax.dot_general(kd, vc, TN_DIMS, preferred_element_type=F32))
            acc_ref[rows, col] += o
        return tuple(new_states)

    lax.fori_loop(0, nchunks, body, tuple(jnp.zeros((LANES, LANES), F32) for _ in streams))

    for hh in range(2):
        col = slice(hh * LANES, (hh + 1) * LANES)
        o = acc_ref[:, col]
        cen = o - jnp.mean(o, axis=-1, keepdims=True)
        var = jnp.mean(cen * cen, axis=-1, keepdims=True)
        y = cen * lax.rsqrt(var + EPS) * gn_ref[:, col]
        gate = rg_ref[0, :, col].astype(F32)
        o_ref[0, :, col] = (y * gate * jax.nn.sigmoid(gate)).astype(BF16)


def _retention(log_gamma, proj3, gn, chunk):
    batch, seq, _ = proj3.shape
    wide = 2 * LANES
    return pl.pallas_call(
        functools.partial(_retention_kernel, chunk=chunk),
        grid=(batch, N_RET_HEADS // 2),
        in_specs=[
            pl.BlockSpec(memory_space=pltpu.SMEM),
            pl.BlockSpec((1, seq, LANES), lambda b, h: (b, 0, 12 + h)),
            pl.BlockSpec((1, seq, LANES), lambda b, h: (b, 0, 14 + h)),
            pl.BlockSpec((1, seq, wide), lambda b, h: (b, 0, 8 + h)),
            pl.BlockSpec((1, seq, wide), lambda b, h: (b, 0, 10 + h)),
            pl.BlockSpec((1, wide), lambda b, h: (0, h)),
        ],
        out_specs=pl.BlockSpec((1, seq, wide), lambda b, h: (b, 0, h)),
        out_shape=jax.ShapeDtypeStruct((batch, seq, N_RET_HEADS * LANES), BF16),
        scratch_shapes=[pltpu.VMEM((seq, wide), F32)],
        compiler_params=_compiler_params(("parallel", "parallel"), 48),
    )(log_gamma, proj3, proj3, proj3, proj3, gn)


def _outproj_kernel(a_ref, r_ref, x_ref, wo_ref, fg_ref, wq_ref, keys_ref, x1_ref, xn_ref, st_ref):
    half = wo_ref.shape[0] // 2
    y = x_ref[...]
    y = y + jnp.dot(a_ref[...], wo_ref[0:half, :], preferred_element_type=F32)
    y = y + jnp.dot(r_ref[...], wo_ref[half:2 * half, :], preferred_element_type=F32)
    x1_ref[...] = y
    xn = (y * lax.rsqrt(jnp.mean(y * y, axis=-1, keepdims=True) + EPS) * fg_ref[...]).astype(BF16)
    xn_ref[...] = xn
    q = jnp.dot(xn, wq_ref[...], preferred_element_type=F32).astype(BF16)
    for hp in range(2 * PEER_HEADS):
        st_ref[hp] = lax.dot_general(keys_ref[hp], q[:, hp * LANES:(hp + 1) * LANES], NT_DIMS,
                                     preferred_element_type=F32)


def _outproj(a2d, r2d, x2d, wo, fg, wq, keys, tm):
    tokens = x2d.shape[0]
    nsets = 2 * PEER_HEADS
    return pl.pallas_call(
        _outproj_kernel,
        grid=(tokens // tm,),
        in_specs=[
            pl.BlockSpec((tm, D_MODEL // 2), lambda i: (i, 0)),
            pl.BlockSpec((tm, D_MODEL // 2), lambda i: (i, 0)),
            pl.BlockSpec((tm, D_MODEL), lambda i: (i, 0)),
            pl.BlockSpec((D_MODEL, D_MODEL), lambda i: (0, 0)),
            pl.BlockSpec((1, D_MODEL), lambda i: (0, 0)),
            pl.BlockSpec((D_MODEL, nsets * LANES), lambda i: (0, 0)),
            pl.BlockSpec((nsets, N_KEYS, LANES), lambda i: (0, 0, 0)),
        ],
        out_specs=[
            pl.BlockSpec((tm, D_MODEL), lambda i: (i, 0)),
            pl.BlockSpec((tm, D_MODEL), lambda i: (i, 0)),
            pl.BlockSpec((nsets, N_KEYS, tm), lambda i: (0, 0, i)),
        ],
        out_shape=[
            jax.ShapeDtypeStruct((tokens, D_MODEL), F32),
            jax.ShapeDtypeStruct((tokens, D_MODEL), BF16),
            jax.ShapeDtypeStruct((nsets, N_KEYS, tokens), F32),
        ],
        compiler_params=_compiler_params(("parallel",), 48),
    )(a2d, r2d, x2d, wo, fg, wq, keys)


def _extract_top(s, code, big, count):
    vals, codes = [], []
    for _ in range(count):
        m = jnp.max(s, axis=0, keepdims=True)
        sel = jnp.min(jnp.where(s == m, code, big), axis=0, keepdims=True)
        vals.append(m)
        codes.append(sel)
        s = jnp.where(code == sel, -jnp.inf, s)
    return jnp.concatenate(vals, axis=0), jnp.concatenate(codes, axis=0)


def _ordered(a, b):
    (va, ca), (vb, cb) = a, b
    a_first = (va > vb) | ((va == vb) & (ca < cb))
    return ((jnp.where(a_first, va, vb), jnp.where(a_first, ca, cb)),
            (jnp.where(a_first, vb, va), jnp.where(a_first, cb, ca)))


def _extract_top_keys(s, row8, count):
    depth, n_stacks = 4, N_KEYS // SUBLANES // 4
    big = float(N_KEYS)
    levels = [[None] * n_stacks for _ in range(depth)]
    for st in range(n_stacks):
        e = [(s[(st * depth + d) * SUBLANES:(st * depth + d + 1) * SUBLANES, :],
              row8 + float((st * depth + d) * SUBLANES)) for d in range(depth)]
        e[0], e[1] = _ordered(e[0], e[1])
        e[2], e[3] = _ordered(e[2], e[3])
        e[0], e[2] = _ordered(e[0], e[2])
        e[1], e[3] = _ordered(e[1], e[3])
        e[1], e[2] = _ordered(e[1], e[2])
        for d in range(depth):
            levels[d][st] = e[d]
    vals, codes = [], []
    for _ in range(count):
        tops = levels[0]
        best = jnp.maximum(jnp.maximum(tops[0][0], tops[1][0]), jnp.maximum(tops[2][0], tops[3][0]))
        m = jnp.max(best, axis=0, keepdims=True)
        tied = [jnp.where(v == m, c, big) for v, c in tops]
        sel = jnp.min(jnp.minimum(jnp.minimum(tied[0], tied[1]), jnp.minimum(tied[2], tied[3])),
                      axis=0, keepdims=True)
        vals.append(m)
        codes.append(sel)
        for st in range(n_stacks):
            hit = levels[0][st][1] == sel
            for d in range(depth - 1):
                (v_up, c_up), (v_dn, c_dn) = levels[d][st], levels[d + 1][st]
                levels[d][st] = (jnp.where(hit, v_dn, v_up), jnp.where(hit, c_dn, c_up))
            v_last, c_last = levels[depth - 1][st]
            levels[depth - 1][st] = (jnp.where(hit, -jnp.inf, v_last), jnp.where(hit, big, c_last))
    return jnp.concatenate(vals, axis=0), jnp.concatenate(codes, axis=0)


def _pick_rows(table, sel):
    out = jnp.zeros_like(sel)
    for a in range(PEER_TOPK):
        out = jnp.where(sel == float(a), table[a:a + 1, :], out)
    return out


def _head_picks(s1, s2):
    k = PEER_TOPK
    tb = s1.shape[1]
    row8 = lax.broadcasted_iota(jnp.int32, (SUBLANES, tb), 0).astype(F32)
    neg = jnp.full((SUBLANES, tb), -jnp.inf, F32)
    v1, i1 = _extract_top_keys(s1, row8, k)
    v2, i2 = _extract_top_keys(s2, row8, k)
    cands, codes = [], []
    for a in range(k // 2):
        nb = k // (a + 1)
        for b0 in range(0, nb, SUBLANES):
            c = v1[a:a + 1, :] + v2[b0:b0 + SUBLANES, :]
            if nb - b0 < SUBLANES:
                c = jnp.where(row8 < float(nb - b0), c, neg)
            cands.append(c)
            codes.append(row8 + float(a * k + b0))
    cands.append(v1[k // 2:k, :] + v2[0:1, :])
    codes.append((row8 + float(k // 2)) * float(k))
    best, code = _extract_top(jnp.concatenate(cands, axis=0), jnp.concatenate(codes, axis=0),
                              float(k * k), k)
    a_sel = jnp.floor(code * (1.0 / k))
    b_sel = code - a_sel * k
    e = jnp.exp(best - best[0:1, :])
    gate = e / jnp.sum(e, axis=0, keepdims=True)
    return _pick_rows(i1, a_sel), _pick_rows(i2, b_sel), gate


def _topk_kernel(st_ref, i_ref, j_ref, g_ref, i_scr, j_scr, g_scr):
    k = PEER_TOPK

    def head_body(h, carry):
        rows = pl.ds(pl.multiple_of(h * k, k), k)
        i_scr[rows, :], j_scr[rows, :], g_scr[rows, :] = _head_picks(st_ref[2 * h], st_ref[2 * h + 1])
        return carry

    lax.fori_loop(0, PEER_HEADS, head_body, 0, unroll=4)
    i_ref[...] = i_scr[...].T
    j_ref[...] = j_scr[...].T
    g_ref[...] = g_scr[...].T


def _topk(st, tb, tokens):
    nsets = st.shape[0]
    slots = PEER_HEADS * PEER_TOPK
    spec = pl.BlockSpec((tb, slots), lambda i: (i, 0))
    shape = jax.ShapeDtypeStruct((tokens, slots), F32)
    return pl.pallas_call(
        _topk_kernel,
        grid=(tokens // tb,),
        in_specs=[pl.BlockSpec((nsets, N_KEYS, tb), lambda i: (0, 0, i))],
        out_specs=[spec, spec, spec],
        out_shape=[shape, shape, shape],
        scratch_shapes=[pltpu.VMEM((slots, tb), F32)] * 3,
        compiler_params=_compiler_params(("parallel",), 32),
    )(st)


def _peer_kernel(xn_ref, i_ref, j_ref, g_ref, st_ref, u_ref, v_ref, x1_ref, fg_ref, o_ref,
                 w_scr, gated_scr, sel_scr, picks_scr, *, row_stride, n_tiles, heads_per_step):
    t = pl.program_id(0)
    e = pl.program_id(1)
    bt = xn_ref.shape[0]
    tile = u_ref.shape[0]
    steps_per_lane_block = PEER_HEADS // heads_per_step

    @pl.when(e == 0)
    def _build_weights():
        @pl.when(t == 0)
        def _first_block_picks():
            picks_scr[0] = i_ref[...]
            picks_scr[1] = j_ref[...]
            picks_scr[2] = g_ref[...]

        o_ref[...] = jnp.zeros_like(o_ref)
        gated_scr[...] = jnp.zeros_like(gated_scr)
        sub = lax.broadcasted_iota(jnp.int32, (N_KEYS, LANES), 0).astype(F32)
        zeros = jnp.zeros((N_KEYS, LANES), BF16)

        def onehots(tok):
            irow = picks_scr[0, pl.ds(tok, 1), :]
            jrow = picks_scr[1, pl.ds(tok, 1), :]
            grow = picks_scr[2, pl.ds(tok, 1), :] * 0.5
            pt = jnp.where(sub == irow, 1.0, 0.0).astype(BF16)
            qt = jnp.where(sub == jrow, grow, 0.0).astype(BF16)
            return pt, qt

        def pair_body(p, carry):
            pt_a, qt_a = onehots(p)
            pt_b, qt_b = onehots(p + bt // 2)
            lhs = jnp.concatenate([pt_a, pt_b], axis=1)
            rhs = jnp.concatenate([jnp.concatenate([qt_a, zeros], axis=1),
                                   jnp.concatenate([zeros, qt_b], axis=1)], axis=0)
            w2 = lax.dot_general(lhs, rhs, NT_DIMS, preferred_element_type=F32)
            w_scr[pl.ds(p, N_KEYS, stride=row_stride), :] = w2[:, :LANES]
            w_scr[pl.ds(p + bt // 2, N_KEYS, stride=row_stride), :] = w2[:, LANES:]
            return carry

        lax.fori_loop(0, bt // 2, pair_body, 0, unroll=16)

    def second_matmul():
        o_ref[...] += jnp.dot(gated_scr[...], v_ref[...], preferred_element_type=F32)

    @pl.when(e < n_tiles)
    def _sweep_step():
        lane_block = e // steps_per_lane_block
        head0 = (e % steps_per_lane_block) * heads_per_step
        for hh in range(heads_per_step):
            picks = _head_picks(st_ref[2 * hh], st_ref[2 * hh + 1])
            rows = pl.ds(pl.multiple_of((head0 + hh) * PEER_TOPK, PEER_TOPK), PEER_TOPK)
            for c in range(3):
                sel_scr[lane_block, c, rows, :] = picks[c]

        second_matmul()

        act = lax.dot_general(xn_ref[...], u_ref[...], NT_DIMS, preferred_element_type=F32)
        i0 = e * (tile // N_KEYS)
        w = jnp.concatenate(
            [w_scr[pl.ds(pl.multiple_of((i0 + ii) * row_stride, SUBLANES), bt), :]
             for ii in range(tile // N_KEYS)], axis=1)
        gated = w * (act * (1.0 + lax.erf(act * (2.0 ** -0.5))))
        gated_scr[...] = gated.astype(BF16)

    @pl.when(e == n_tiles)
    def _finish():
        second_matmul()
        y = x1_ref[...] + o_ref[...]
        o_ref[...] = y * lax.rsqrt(jnp.mean(y * y, axis=-1, keepdims=True) + EPS) * fg_ref[...]
        for lb in range(bt // LANES):
            for c in range(3):
                picks_scr[c, lb * LANES:(lb + 1) * LANES, :] = sel_scr[lb, c].T


def _peer(xn, i_first, j_first, g_first, scores_t, u_bf16, v_bf16, x1, fg, bt, tile):
    tokens = xn.shape[0]
    n_blocks = tokens // bt
    n_tiles = u_bf16.shape[0] // tile
    lane_blocks = bt // LANES
    steps_per_lane_block = n_tiles // lane_blocks
    heads_per_step = PEER_HEADS // steps_per_lane_block
    assert (n_tiles == lane_blocks * steps_per_lane_block
            and PEER_HEADS == heads_per_step * steps_per_lane_block)
    row_stride = bt + SUBLANES
    slots = PEER_HEADS * PEER_TOPK
    tok = lambda width: pl.BlockSpec((bt, width), lambda t, e: (t, 0))
    first = pl.BlockSpec((bt, slots), lambda t, e: (0, 0))

    def scores_index(t, e):
        ec = jnp.minimum(e, n_tiles - 1)
        nxt = jnp.minimum(t + 1, n_blocks - 1)
        return (ec % steps_per_lane_block, 0, nxt * lane_blocks + ec // steps_per_lane_block)

    return pl.pallas_call(
        functools.partial(_peer_kernel, row_stride=row_stride, n_tiles=n_tiles,
                          heads_per_step=heads_per_step),
        grid=(n_blocks, n_tiles + 1),
        in_specs=[
            tok(D_MODEL), first, first, first,
            pl.BlockSpec((2 * heads_per_step, N_KEYS, LANES), scores_index),
            pl.BlockSpec((tile, D_MODEL), lambda t, e: (jnp.minimum(e, n_tiles - 1), 0)),
            pl.BlockSpec((tile, D_MODEL), lambda t, e: (jnp.maximum(e - 1, 0), 0)),
            tok(D_MODEL),
            pl.BlockSpec((1, D_MODEL), lambda t, e: (0, 0)),
        ],
        out_specs=tok(D_MODEL),
        out_shape=jax.ShapeDtypeStruct((tokens, D_MODEL), F32),
        scratch_shapes=[pltpu.VMEM((N_KEYS * row_stride, LANES), F32),
                        pltpu.VMEM((bt, tile), BF16),
                        pltpu.VMEM((lane_blocks, 3, slots, LANES), F32),
                        pltpu.VMEM((3, bt, slots), F32)],
        compiler_params=_compiler_params(("arbitrary", "arbitrary"), 60),
    )(xn, i_first, j_first, g_first, scores_t, u_bf16, v_bf16, x1, fg)


def kernel(x, attn_norm_g, w_in, diff_lambda, diff_norm_g, ret_log_decay, ret_norm_g, w_out,
           ffn_norm_g, peer_w_query, peer_sub_keys, peer_u, peer_v, final_norm_g):
    batch, seq, d_model = x.shape
    assert d_model == D_MODEL and seq % LANES == 0
    tokens = batch * seq
    tm = min(512, seq)
    x2d = x.reshape(tokens, D_MODEL)

    rot_dim = DIFF_QK_DIM // ROPE_FRACTION
    rope_inv = jnp.power(jnp.float32(ROPE_THETA), -jnp.arange(rot_dim // 2, dtype=F32) * 2.0 / rot_dim)
    ret_inv = 1.0 / jnp.power(jnp.float32(RET_THETA), jnp.linspace(0.0, 1.0, RET_QK_DIM // 2, dtype=F32))
    dtab = _rotary_tables(seq, rope_inv, rot_dim // 2)
    rtab = _rotary_tables(seq, ret_inv, RET_QK_DIM // 2)

    proj = _inproj(x2d, attn_norm_g[0].reshape(1, D_MODEL), w_in[0].astype(BF16), dtab, rtab, seq, tm)
    proj3 = proj.reshape(batch, seq, IN_COLS)

    lam_p = diff_lambda[0].astype(F32)
    lam = (jnp.exp(jnp.sum(lam_p[0] * lam_p[1])) - jnp.exp(jnp.sum(lam_p[2] * lam_p[3]))
           + LAMBDA_INIT).reshape(1).astype(F32)
    attn = _diffattn(lam, proj3, diff_norm_g[0].reshape(1, LANES), min(512, seq))

    log_gamma = -jnp.exp(ret_log_decay[0].astype(F32))
    ret = _retention(log_gamma, proj3, ret_norm_g[0].reshape(1, N_RET_HEADS * LANES), min(256, seq))

    x1, xn, scores_t = _outproj(
        attn.reshape(tokens, D_MODEL // 2), ret.reshape(tokens, D_MODEL // 2), x2d,
        w_out[0].astype(BF16), ffn_norm_g[0].reshape(1, D_MODEL), peer_w_query[0].astype(BF16),
        peer_sub_keys[0].reshape(2 * PEER_HEADS, N_KEYS, LANES).astype(BF16), tm)

    bt = min(512, tokens)
    i_first, j_first, g_first = _topk(scores_t, LANES, bt)

    y = _peer(xn, i_first, j_first, g_first, scores_t, peer_u[0].astype(BF16),
              peer_v[0].astype(BF16), x1, final_norm_g.reshape(1, D_MODEL), bt, 1024)
    return y.reshape(batch, seq, D_MODEL)
```

```python
import functools
import math

import jax
import jax.numpy as jnp
from jax import lax
from jax.experimental import pallas as pl
from jax.experimental.pallas import tpu as pltpu

F32 = jnp.float32
BF16 = jnp.bfloat16

D_MODEL = 1024
N_DIFF_HEADS = 4
DIFF_QK_DIM = 64
N_RET_HEADS = 4
RET_QK_DIM = 64
IN_COLS = 3072
ROPE_THETA = 500000.0
ROPE_FRACTION = 4
RET_THETA = 10000.0
N_KEYS = 128
PEER_HEADS = 8
PEER_TOPK = 16
EPS = 1e-6
LAMBDA_INIT = 0.8 - 0.6 * math.exp(0.0)

LANES = 128
SUBLANES = 8
MXU_DIM = 256
VMEM_BYTES_V7X = 64 * 1024 * 1024

ROW_BLOCK = 512
ATTN_Q_BLOCK = 1024
ATTN_KV_CHUNK = 256
RET_CHUNK = 256
TOPK_LANE_BLOCK = LANES
PEER_TOKEN_BLOCK = 512
PEER_EXPERT_TILE = 1024

VMEM_LIMIT_PEER = VMEM_BYTES_V7X - 4 * 1024 * 1024
VMEM_LIMIT_DEFAULT = VMEM_BYTES_V7X * 3 // 4

NT_DIMS = (((1,), (1,)), ((), ()))
TN_DIMS = (((0,), (0,)), ((), ()))


def _compiler_params(semantics, vmem_bytes=VMEM_LIMIT_DEFAULT):
    return pltpu.CompilerParams(dimension_semantics=semantics, vmem_limit_bytes=vmem_bytes)


def _rotary_tables(seq, inv_freq, half):
    pos = jnp.arange(seq, dtype=F32)
    lane = jnp.arange(LANES)
    lp = lane % 64
    rotated = lp < 2 * half
    freq = jnp.where(rotated, lp % half, 0)
    ang = pos[:, None] * inv_freq[freq][None, :]
    cos = jnp.where(rotated[None, :], jnp.cos(ang), 1.0)
    sin = jnp.sin(ang)
    sin_lo = jnp.where((lp < half)[None, :], -sin, 0.0)
    sin_hi = jnp.where((rotated & (lp >= half))[None, :], sin, 0.0)
    return jnp.stack([cos, sin_lo, sin_hi]).astype(F32)


def _rotate(p, tab_ref, half):
    up = pltpu.roll(p, LANES - half, axis=1)
    dn = pltpu.roll(p, half, axis=1)
    return p * tab_ref[0] + up * tab_ref[1] + dn * tab_ref[2]


def _inproj_kernel(x_ref, g_ref, w_ref, dtab_ref, rtab_ref, o_ref):
    x = x_ref[...]
    h = (x * lax.rsqrt(jnp.mean(x * x, axis=-1, keepdims=True) + EPS) * g_ref[...]).astype(BF16)
    chunk = 512
    for c in range(IN_COLS // chunk):
        p = jnp.dot(h, w_ref[:, c * chunk:(c + 1) * chunk], preferred_element_type=F32)
        for j in range(chunk // LANES):
            slab = c * (chunk // LANES) + j
            ps = p[:, j * LANES:(j + 1) * LANES]
            if slab < 8:
                ps = _rotate(ps, dtab_ref, DIFF_QK_DIM // ROPE_FRACTION // 2)
                if slab < 4:
                    ps = ps * (DIFF_QK_DIM ** -0.5 * math.log2(math.e))
            elif 12 <= slab < 16:
                ps = _rotate(ps, rtab_ref, RET_QK_DIM // 2)
                if slab >= 14:
                    ps = ps * (RET_QK_DIM ** -0.5)
            o_ref[:, slab * LANES:(slab + 1) * LANES] = ps.astype(BF16)


def _inproj(x2d, g, w_bf16, dtab, rtab, seq, tm):
    tokens = x2d.shape[0]
    nseq = seq // tm
    return pl.pallas_call(
        _inproj_kernel,
        grid=(tokens // tm,),
        in_specs=[
            pl.BlockSpec((tm, D_MODEL), lambda i: (i, 0)),
            pl.BlockSpec((1, D_MODEL), lambda i: (0, 0)),
            pl.BlockSpec((D_MODEL, IN_COLS), lambda i: (0, 0)),
            pl.BlockSpec((3, tm, LANES), lambda i: (0, i % nseq, 0)),
            pl.BlockSpec((3, tm, LANES), lambda i: (0, i % nseq, 0)),
        ],
        out_specs=pl.BlockSpec((tm, IN_COLS), lambda i: (i, 0)),
        out_shape=jax.ShapeDtypeStruct((tokens, IN_COLS), BF16),
        compiler_params=_compiler_params(("parallel",)),
    )(x2d, g, w_bf16, dtab, rtab)


def _diffattn_kernel(lam_ref, q_ref, k_ref, v_ref, g_ref, o_ref):
    q = q_ref[0]
    k = k_ref[0]
    v = v_ref[0]
    lane = lax.broadcasted_iota(jnp.int32, q.shape, 1)
    zero = jnp.zeros_like(q)
    q1 = jnp.where(lane < DIFF_QK_DIM, q, zero)
    q2 = jnp.where(lane >= DIFF_QK_DIM, q, zero)

    v_ones = jnp.concatenate([v, jnp.ones_like(v)], axis=1)
    seq = k.shape[0]
    kv_chunk = min(seq, ATTN_KV_CHUNK)

    queries = (q1, q2)
    ms = [jnp.full((q.shape[0], 1), -jnp.inf, F32) for _ in queries]
    accs = [jnp.zeros((q.shape[0], 2 * LANES), F32) for _ in queries]
    for c in range(seq // kv_chunk):
        rows = slice(c * kv_chunk, (c + 1) * kv_chunk)
        for i, qm in enumerate(queries):
            s = lax.dot_general(qm, k[rows], NT_DIMS, preferred_element_type=F32)
            m_new = jnp.maximum(ms[i], jnp.max(s, axis=-1, keepdims=True))
            p = jnp.exp2(s - m_new).astype(BF16)
            accs[i] = (accs[i] * jnp.exp2(ms[i] - m_new)
                       + jnp.dot(p, v_ones[rows], preferred_element_type=F32))
            ms[i] = m_new
    outs = [acc[:, :LANES] / acc[:, LANES:] for acc in accs]

    o = outs[0] - lam_ref[0] * outs[1]
    o = o * lax.rsqrt(jnp.mean(o * o, axis=-1, keepdims=True) + EPS) * g_ref[...]
    o_ref[0] = (o * (1.0 - LAMBDA_INIT)).astype(BF16)


def _diffattn(lam, proj3, g, tq):
    batch, seq, _ = proj3.shape
    return pl.pallas_call(
        _diffattn_kernel,
        grid=(batch, N_DIFF_HEADS, seq // tq),
        in_specs=[
            pl.BlockSpec(memory_space=pltpu.SMEM),
            pl.BlockSpec((1, tq, LANES), lambda b, h, i: (b, i, h)),
            pl.BlockSpec((1, seq, LANES), lambda b, h, i: (b, 0, 4 + h)),
            pl.BlockSpec((1, seq, LANES), lambda b, h, i: (b, 0, 8 + h)),
            pl.BlockSpec((1, LANES), lambda b, h, i: (0, 0)),
        ],
        out_specs=pl.BlockSpec((1, tq, LANES), lambda b, h, i: (b, i, h)),
        out_shape=jax.ShapeDtypeStruct((batch, seq, N_DIFF_HEADS * LANES), BF16),
        compiler_params=_compiler_params(("parallel", "parallel", "parallel")),
    )(lam, proj3, proj3, proj3, g)


def _retention_kernel(lg_ref, q_ref, k_ref, v_ref, rg_ref, gn_ref, o_ref, acc_ref, *, chunk):
    hp = pl.program_id(1)
    seq = q_ref.shape[1]
    nchunks = seq // chunk
    lane = lax.broadcasted_iota(jnp.int32, (chunk, LANES), 1)
    ri = lax.broadcasted_iota(jnp.int32, (chunk, chunk), 0).astype(F32)
    ci = lax.broadcasted_iota(jnp.int32, (chunk, chunk), 1).astype(F32)
    pos = lax.broadcasted_iota(jnp.int32, (chunk, 1), 0).astype(F32)

    streams = []
    for hh in range(2):
        hmask = (lane >= RET_QK_DIM * hh) & (lane < RET_QK_DIM * (hh + 1))
        col = slice(hh * LANES, (hh + 1) * LANES)
        for direction in range(2):
            lg = lg_ref[direction, 2 * hp + hh]
            if direction == 0:
                dist = ri - ci
                valid = dist >= 0.0
                q_decay = jnp.exp(lg * (pos + 1.0))
                k_decay = jnp.exp(lg * (chunk - 1.0 - pos))
            else:
                dist = ci - ri
                valid = dist > 0.0
                q_decay = jnp.exp(lg * (chunk - pos))
                k_decay = jnp.exp(lg * pos)
            decay = jnp.where(valid, jnp.exp(lg * jnp.maximum(dist, 0.0)), 0.0)
            chunk_decay = jnp.exp(jnp.full((1, 1), lg * chunk, F32))
            streams.append((direction, hmask, col, decay, q_decay, k_decay, chunk_decay))

    acc_ref[...] = jnp.zeros_like(acc_ref)

    def body(n, states):
        new_states = []
        for state, (direction, hmask, col, decay, q_decay, k_decay, chunk_decay) in zip(states, streams):
            c = n if direction == 0 else nchunks - 1 - n
            rows = pl.ds(pl.multiple_of(c * chunk, chunk), chunk)
            qc = q_ref[0, rows, :]
            kc = k_ref[0, rows, :]
            vc = v_ref[0, rows, col]
            qh = jnp.where(hmask, qc, jnp.zeros_like(qc))
            inner = lax.dot_general(qh, kc, NT_DIMS, preferred_element_type=F32) * decay
            o = jnp.dot(inner.astype(BF16), vc, preferred_element_type=F32)
            o = o + jnp.dot((qh.astype(F32) * q_decay).astype(BF16), state.astype(BF16),
                            preferred_element_type=F32)
            kd = (kc.astype(F32) * k_decay).astype(BF16)
            new_states.append(state * chunk_decay
                              + lax.dot_general(kd, vc, TN_DIMS, preferred_element_type=F32))
            acc_ref[rows, col] += o
        return tuple(new_states)

    lax.fori_loop(0, nchunks, body, tuple(jnp.zeros((LANES, LANES), F32) for _ in streams))

    for hh in range(2):
        col = slice(hh * LANES, (hh + 1) * LANES)
        o = acc_ref[:, col]
        cen = o - jnp.mean(o, axis=-1, keepdims=True)
        var = jnp.mean(cen * cen, axis=-1, keepdims=True)
        y = cen * lax.rsqrt(var + EPS) * gn_ref[:, col]
        gate = rg_ref[0, :, col].astype(F32)
        o_ref[0, :, col] = (y * gate * jax.nn.sigmoid(gate)).astype(BF16)


def _retention(log_gamma, proj3, gn, chunk):
    batch, seq, _ = proj3.shape
    wide = 2 * LANES
    return pl.pallas_call(
        functools.partial(_retention_kernel, chunk=chunk),
        grid=(batch, N_RET_HEADS // 2),
        in_specs=[
            pl.BlockSpec(memory_space=pltpu.SMEM),
            pl.BlockSpec((1, seq, LANES), lambda b, h: (b, 0, 12 + h)),
            pl.BlockSpec((1, seq, LANES), lambda b, h: (b, 0, 14 + h)),
            pl.BlockSpec((1, seq, wide), lambda b, h: (b, 0, 8 + h)),
            pl.BlockSpec((1, seq, wide), lambda b, h: (b, 0, 10 + h)),
            pl.BlockSpec((1, wide), lambda b, h: (0, h)),
        ],
        out_specs=pl.BlockSpec((1, seq, wide), lambda b, h: (b, 0, h)),
        out_shape=jax.ShapeDtypeStruct((batch, seq, N_RET_HEADS * LANES), BF16),
        scratch_shapes=[pltpu.VMEM((seq, wide), F32)],
        compiler_params=_compiler_params(("parallel", "parallel")),
    )(log_gamma, proj3, proj3, proj3, proj3, gn)


def _outproj_kernel(a_ref, r_ref, x_ref, wo_ref, fg_ref, wq_ref, keys_ref, x1_ref, xn_ref, st_ref):
    half = wo_ref.shape[0] // 2
    y = x_ref[...]
    y = y + jnp.dot(a_ref[...], wo_ref[0:half, :], preferred_element_type=F32)
    y = y + jnp.dot(r_ref[...], wo_ref[half:2 * half, :], preferred_element_type=F32)
    x1_ref[...] = y
    xn = (y * lax.rsqrt(jnp.mean(y * y, axis=-1, keepdims=True) + EPS) * fg_ref[...]).astype(BF16)
    xn_ref[...] = xn
    q = jnp.dot(xn, wq_ref[...], preferred_element_type=F32).astype(BF16)
    for hp in range(2 * PEER_HEADS):
        st_ref[hp] = lax.dot_general(keys_ref[hp], q[:, hp * LANES:(hp + 1) * LANES], NT_DIMS,
                                     preferred_element_type=F32)


def _outproj(a2d, r2d, x2d, wo, fg, wq, keys, tm):
    tokens = x2d.shape[0]
    nsets = 2 * PEER_HEADS
    return pl.pallas_call(
        _outproj_kernel,
        grid=(tokens // tm,),
        in_specs=[
            pl.BlockSpec((tm, D_MODEL // 2), lambda i: (i, 0)),
            pl.BlockSpec((tm, D_MODEL // 2), lambda i: (i, 0)),
            pl.BlockSpec((tm, D_MODEL), lambda i: (i, 0)),
            pl.BlockSpec((D_MODEL, D_MODEL), lambda i: (0, 0)),
            pl.BlockSpec((1, D_MODEL), lambda i: (0, 0)),
            pl.BlockSpec((D_MODEL, nsets * LANES), lambda i: (0, 0)),
            pl.BlockSpec((nsets, N_KEYS, LANES), lambda i: (0, 0, 0)),
        ],
        out_specs=[
            pl.BlockSpec((tm, D_MODEL), lambda i: (i, 0)),
            pl.BlockSpec((tm, D_MODEL), lambda i: (i, 0)),
            pl.BlockSpec((nsets, N_KEYS, tm), lambda i: (0, 0, i)),
        ],
        out_shape=[
            jax.ShapeDtypeStruct((tokens, D_MODEL), F32),
            jax.ShapeDtypeStruct((tokens, D_MODEL), BF16),
            jax.ShapeDtypeStruct((nsets, N_KEYS, tokens), F32),
        ],
        compiler_params=_compiler_params(("parallel",)),
    )(a2d, r2d, x2d, wo, fg, wq, keys)


def _extract_top(s, code, big, count):
    vals, codes = [], []
    for _ in range(count):
        m = jnp.max(s, axis=0, keepdims=True)
        sel = jnp.min(jnp.where(s == m, code, big), axis=0, keepdims=True)
        vals.append(m)
        codes.append(sel)
        s = jnp.where(code == sel, -jnp.inf, s)
    return jnp.concatenate(vals, axis=0), jnp.concatenate(codes, axis=0)


def _ordered(a, b):
    (va, ca), (vb, cb) = a, b
    a_first = (va > vb) | ((va == vb) & (ca < cb))
    return ((jnp.where(a_first, va, vb), jnp.where(a_first, ca, cb)),
            (jnp.where(a_first, vb, va), jnp.where(a_first, cb, ca)))


def _extract_top_keys(s, row8, count):
    depth, n_stacks = 4, N_KEYS // SUBLANES // 4
    big = float(N_KEYS)
    levels = [[None] * n_stacks for _ in range(depth)]
    for st in range(n_stacks):
        e = [(s[(st * depth + d) * SUBLANES:(st * depth + d + 1) * SUBLANES, :],
              row8 + float((st * depth + d) * SUBLANES)) for d in range(depth)]
        e[0], e[1] = _ordered(e[0], e[1])
        e[2], e[3] = _ordered(e[2], e[3])
        e[0], e[2] = _ordered(e[0], e[2])
        e[1], e[3] = _ordered(e[1], e[3])
        e[1], e[2] = _ordered(e[1], e[2])
        for d in range(depth):
            levels[d][st] = e[d]
    vals, codes = [], []
    for _ in range(count):
        tops = levels[0]
        best = jnp.maximum(jnp.maximum(tops[0][0], tops[1][0]), jnp.maximum(tops[2][0], tops[3][0]))
        m = jnp.max(best, axis=0, keepdims=True)
        tied = [jnp.where(v == m, c, big) for v, c in tops]
        sel = jnp.min(jnp.minimum(jnp.minimum(tied[0], tied[1]), jnp.minimum(tied[2], tied[3])),
                      axis=0, keepdims=True)
        vals.append(m)
        codes.append(sel)
        for st in range(n_stacks):
            hit = levels[0][st][1] == sel
            for d in range(depth - 1):
                (v_up, c_up), (v_dn, c_dn) = levels[d][st], levels[d + 1][st]
                levels[d][st] = (jnp.where(hit, v_dn, v_up), jnp.where(hit, c_dn, c_up))
            v_last, c_last = levels[depth - 1][st]
            levels[depth - 1][st] = (jnp.where(hit, -jnp.inf, v_last), jnp.where(hit, big, c_last))
    return jnp.concatenate(vals, axis=0), jnp.concatenate(codes, axis=0)


def _pick_rows(table, sel):
    out = jnp.zeros_like(sel)
    for a in range(PEER_TOPK):
        out = jnp.where(sel == float(a), table[a:a + 1, :], out)
    return out


def _head_picks(s1, s2):
    k = PEER_TOPK
    tb = s1.shape[1]
    row8 = lax.broadcasted_iota(jnp.int32, (SUBLANES, tb), 0).astype(F32)
    neg = jnp.full((SUBLANES, tb), -jnp.inf, F32)
    v1, i1 = _extract_top_keys(s1, row8, k)
    v2, i2 = _extract_top_keys(s2, row8, k)
    cands, codes = [], []
    for a in range(k // 2):
        nb = k // (a + 1)
        for b0 in range(0, nb, SUBLANES):
            c = v1[a:a + 1, :] + v2[b0:b0 + SUBLANES, :]
            if nb - b0 < SUBLANES:
                c = jnp.where(row8 < float(nb - b0), c, neg)
            cands.append(c)
            codes.append(row8 + float(a * k + b0))
    cands.append(v1[k // 2:k, :] + v2[0:1, :])
    codes.append((row8 + float(k // 2)) * float(k))
    best, code = _extract_top(jnp.concatenate(cands, axis=0), jnp.concatenate(codes, axis=0),
                              float(k * k), k)
    a_sel = jnp.floor(code * (1.0 / k))
    b_sel = code - a_sel * k
    e = jnp.exp(best - best[0:1, :])
    gate = e / jnp.sum(e, axis=0, keepdims=True)
    return _pick_rows(i1, a_sel), _pick_rows(i2, b_sel), gate


def _topk_kernel(st_ref, i_ref, j_ref, g_ref, i_scr, j_scr, g_scr):
    k = PEER_TOPK

    def head_body(h, carry):
        rows = pl.ds(pl.multiple_of(h * k, k), k)
        i_scr[rows, :], j_scr[rows, :], g_scr[rows, :] = _head_picks(st_ref[2 * h], st_ref[2 * h + 1])
        return carry

    lax.fori_loop(0, PEER_HEADS, head_body, 0, unroll=4)
    i_ref[...] = i_scr[...].T
    j_ref[...] = j_scr[...].T
    g_ref[...] = g_scr[...].T


def _topk(st, tb, tokens):
    nsets = st.shape[0]
    slots = PEER_HEADS * PEER_TOPK
    spec = pl.BlockSpec((tb, slots), lambda i: (i, 0))
    shape = jax.ShapeDtypeStruct((tokens, slots), F32)
    return pl.pallas_call(
        _topk_kernel,
        grid=(tokens // tb,),
        in_specs=[pl.BlockSpec((nsets, N_KEYS, tb), lambda i: (0, 0, i))],
        out_specs=[spec, spec, spec],
        out_shape=[shape, shape, shape],
        scratch_shapes=[pltpu.VMEM((slots, tb), F32)] * 3,
        compiler_params=_compiler_params(("parallel",)),
    )(st)


def _peer_kernel(xn_ref, i_ref, j_ref, g_ref, st_ref, u_ref, v_ref, x1_ref, fg_ref, o_ref,
                 w_scr, gated_scr, sel_scr, picks_scr, *, row_stride, n_tiles, heads_per_step):
    t = pl.program_id(0)
    e = pl.program_id(1)
    bt = xn_ref.shape[0]
    tile = u_ref.shape[0]
    steps_per_lane_block = PEER_HEADS // heads_per_step

    @pl.when(e == 0)
    def _build_weights():
        @pl.when(t == 0)
        def _first_block_picks():
            picks_scr[0] = i_ref[...]
            picks_scr[1] = j_ref[...]
            picks_scr[2] = g_ref[...]

        o_ref[...] = jnp.zeros_like(o_ref)
        gated_scr[...] = jnp.zeros_like(gated_scr)
        sub = lax.broadcasted_iota(jnp.int32, (N_KEYS, LANES), 0).astype(F32)
        zeros = jnp.zeros((N_KEYS, LANES), BF16)

        def onehots(tok):
            irow = picks_scr[0, pl.ds(tok, 1), :]
            jrow = picks_scr[1, pl.ds(tok, 1), :]
            grow = picks_scr[2, pl.ds(tok, 1), :] * 0.5
            pt = jnp.where(sub == irow, 1.0, 0.0).astype(BF16)
            qt = jnp.where(sub == jrow, grow, 0.0).astype(BF16)
            return pt, qt

        def pair_body(p, carry):
            pt_a, qt_a = onehots(p)
            pt_b, qt_b = onehots(p + bt // 2)
            lhs = jnp.concatenate([pt_a, pt_b], axis=1)
            rhs = jnp.concatenate([jnp.concatenate([qt_a, zeros], axis=1),
                                   jnp.concatenate([zeros, qt_b], axis=1)], axis=0)
            w2 = lax.dot_general(lhs, rhs, NT_DIMS, preferred_element_type=F32)
            w_scr[pl.ds(p, N_KEYS, stride=row_stride), :] = w2[:, :LANES]
            w_scr[pl.ds(p + bt // 2, N_KEYS, stride=row_stride), :] = w2[:, LANES:]
            return carry

        lax.fori_loop(0, bt // 2, pair_body, 0, unroll=16)

    def second_matmul():
        o_ref[...] += jnp.dot(gated_scr[...], v_ref[...], preferred_element_type=F32)

    @pl.when(e < n_tiles)
    def _sweep_step():
        lane_block = e // steps_per_lane_block
        head0 = (e % steps_per_lane_block) * heads_per_step
        for hh in range(heads_per_step):
            picks = _head_picks(st_ref[2 * hh], st_ref[2 * hh + 1])
            rows = pl.ds(pl.multiple_of((head0 + hh) * PEER_TOPK, PEER_TOPK), PEER_TOPK)
            for c in range(3):
                sel_scr[lane_block, c, rows, :] = picks[c]

        second_matmul()

        act = lax.dot_general(xn_ref[...], u_ref[...], NT_DIMS, preferred_element_type=F32)
        i0 = e * (tile // N_KEYS)
        w = jnp.concatenate(
            [w_scr[pl.ds(pl.multiple_of((i0 + ii) * row_stride, SUBLANES), bt), :]
             for ii in range(tile // N_KEYS)], axis=1)
        gated = w * (act * (1.0 + lax.erf(act * (2.0 ** -0.5))))
        gated_scr[...] = gated.astype(BF16)

    @pl.when(e == n_tiles)
    def _finish():
        second_matmul()
        y = x1_ref[...] + o_ref[...]
        o_ref[...] = y * lax.rsqrt(jnp.mean(y * y, axis=-1, keepdims=True) + EPS) * fg_ref[...]
        for lb in range(bt // LANES):
            for c in range(3):
                picks_scr[c, lb * LANES:(lb + 1) * LANES, :] = sel_scr[lb, c].T


def _peer(xn, i_first, j_first, g_first, scores_t, u_bf16, v_bf16, x1, fg, bt, tile):
    tokens = xn.shape[0]
    n_blocks = tokens // bt
    n_tiles = u_bf16.shape[0] // tile
    lane_blocks = bt // LANES
    steps_per_lane_block = n_tiles // lane_blocks
    heads_per_step = PEER_HEADS // steps_per_lane_block
    assert (n_tiles == lane_blocks * steps_per_lane_block
            and PEER_HEADS == heads_per_step * steps_per_lane_block)
    row_stride = bt + SUBLANES
    slots = PEER_HEADS * PEER_TOPK
    tok = lambda width: pl.BlockSpec((bt, width), lambda t, e: (t, 0))
    first = pl.BlockSpec((bt, slots), lambda t, e: (0, 0))

    def scores_index(t, e):
        ec = jnp.minimum(e, n_tiles - 1)
        nxt = jnp.minimum(t + 1, n_blocks - 1)
        return (ec % steps_per_lane_block, 0, nxt * lane_blocks + ec // steps_per_lane_block)

    return pl.pallas_call(
        functools.partial(_peer_kernel, row_stride=row_stride, n_tiles=n_tiles,
                          heads_per_step=heads_per_step),
        grid=(n_blocks, n_tiles + 1),
        in_specs=[
            tok(D_MODEL), first, first, first,
            pl.BlockSpec((2 * heads_per_step, N_KEYS, LANES), scores_index),
            pl.BlockSpec((tile, D_MODEL), lambda t, e: (jnp.minimum(e, n_tiles - 1), 0)),
            pl.BlockSpec((tile, D_MODEL), lambda t, e: (jnp.maximum(e - 1, 0), 0)),
            tok(D_MODEL),
            pl.BlockSpec((1, D_MODEL), lambda t, e: (0, 0)),
        ],
        out_specs=tok(D_MODEL),
        out_shape=jax.ShapeDtypeStruct((tokens, D_MODEL), F32),
        scratch_shapes=[pltpu.VMEM((N_KEYS * row_stride, LANES), F32),
                        pltpu.VMEM((bt, tile), BF16),
                        pltpu.VMEM((lane_blocks, 3, slots, LANES), F32),
                        pltpu.VMEM((3, bt, slots), F32)],
        compiler_params=_compiler_params(("arbitrary", "arbitrary"), VMEM_LIMIT_PEER),
    )(xn, i_first, j_first, g_first, scores_t, u_bf16, v_bf16, x1, fg)


def kernel(x, attn_norm_g, w_in, diff_lambda, diff_norm_g, ret_log_decay, ret_norm_g, w_out,
           ffn_norm_g, peer_w_query, peer_sub_keys, peer_u, peer_v, final_norm_g):
    batch, seq, d_model = x.shape
    assert d_model == D_MODEL and seq % LANES == 0
    tokens = batch * seq
    tm = min(ROW_BLOCK, seq)
    x2d = x.reshape(tokens, D_MODEL)

    rot_dim = DIFF_QK_DIM // ROPE_FRACTION
    rope_inv = jnp.power(jnp.float32(ROPE_THETA), -jnp.arange(rot_dim // 2, dtype=F32) * 2.0 / rot_dim)
    ret_inv = 1.0 / jnp.power(jnp.float32(RET_THETA), jnp.linspace(0.0, 1.0, RET_QK_DIM // 2, dtype=F32))
    dtab = _rotary_tables(seq, rope_inv, rot_dim // 2)
    rtab = _rotary_tables(seq, ret_inv, RET_QK_DIM // 2)

    proj = _inproj(x2d, attn_norm_g[0].reshape(1, D_MODEL), w_in[0].astype(BF16), dtab, rtab, seq, tm)
    proj3 = proj.reshape(batch, seq, IN_COLS)

    lam_p = diff_lambda[0].astype(F32)
    lam = (jnp.exp(jnp.sum(lam_p[0] * lam_p[1])) - jnp.exp(jnp.sum(lam_p[2] * lam_p[3]))
           + LAMBDA_INIT).reshape(1).astype(F32)
    attn = _diffattn(lam, proj3, diff_norm_g[0].reshape(1, LANES),
                     min(ATTN_Q_BLOCK, seq))

    log_gamma = -jnp.exp(ret_log_decay[0].astype(F32))
    ret = _retention(log_gamma, proj3, ret_norm_g[0].reshape(1, N_RET_HEADS * LANES),
                     min(RET_CHUNK, seq))

    x1, xn, scores_t = _outproj(
        attn.reshape(tokens, D_MODEL // 2), ret.reshape(tokens, D_MODEL // 2), x2d,
        w_out[0].astype(BF16), ffn_norm_g[0].reshape(1, D_MODEL), peer_w_query[0].astype(BF16),
        peer_sub_keys[0].reshape(2 * PEER_HEADS, N_KEYS, LANES).astype(BF16), tm)

    bt = min(PEER_TOKEN_BLOCK, tokens)
    i_first, j_first, g_first = _topk(scores_t, TOPK_LANE_BLOCK, bt)

    y = _peer(xn, i_first, j_first, g_first, scores_t, peer_u[0].astype(BF16),
              peer_v[0].astype(BF16), x1, final_norm_g.reshape(1, D_MODEL), bt, PEER_EXPERT_TILE)
    return y.reshape(batch, seq, D_MODEL)
```

```python
import functools
import math

import jax
import jax.numpy as jnp
from jax import lax
from jax.experimental import pallas as pl
from jax.experimental.pallas import tpu as pltpu

F32 = jnp.float32
BF16 = jnp.bfloat16

D_MODEL = 1024
N_DIFF_HEADS = 4
DIFF_QK_DIM = 64
N_RET_HEADS = 4
RET_QK_DIM = 64
IN_COLS = 3072
ROPE_THETA = 500000.0
ROPE_FRACTION = 4
RET_THETA = 10000.0
N_KEYS = 128
PEER_HEADS = 8
PEER_TOPK = 16
EPS = 1e-6
LAMBDA_INIT = 0.8 - 0.6 * math.exp(0.0)

LANES = 128
SUBLANES = 8
MXU_DIM = 256
VMEM_BYTES_V7X = 64 * 1024 * 1024

ROW_BLOCK = 512
ATTN_Q_BLOCK = 1024
ATTN_KV_CHUNK = 256
RET_CHUNK = 256
TOPK_LANE_BLOCK = LANES
PEER_TOKEN_BLOCK = 512
PEER_EXPERT_TILE = 1024

VMEM_LIMIT_PEER = VMEM_BYTES_V7X - 4 * 1024 * 1024
VMEM_LIMIT_DEFAULT = VMEM_BYTES_V7X * 3 // 4

NT_DIMS = (((1,), (1,)), ((), ()))
TN_DIMS = (((0,), (0,)), ((), ()))


def _compiler_params(semantics, vmem_bytes=VMEM_LIMIT_DEFAULT):
    return pltpu.CompilerParams(dimension_semantics=semantics, vmem_limit_bytes=vmem_bytes)


def _rotary_tables(seq, inv_freq, half):
    pos = jnp.arange(seq, dtype=F32)
    lane = jnp.arange(LANES)
    lp = lane % 64
    rotated = lp < 2 * half
    freq = jnp.where(rotated, lp % half, 0)
    ang = pos[:, None] * inv_freq[freq][None, :]
    cos = jnp.where(rotated[None, :], jnp.cos(ang), 1.0)
    sin = jnp.sin(ang)
    sin_lo = jnp.where((lp < half)[None, :], -sin, 0.0)
    sin_hi = jnp.where((rotated & (lp >= half))[None, :], sin, 0.0)
    return jnp.stack([cos, sin_lo, sin_hi]).astype(F32)


def _rotate(p, tab_ref, half):
    up = pltpu.roll(p, LANES - half, axis=1)
    dn = pltpu.roll(p, half, axis=1)
    return p * tab_ref[0] + up * tab_ref[1] + dn * tab_ref[2]


def _inproj_kernel(x_ref, g_ref, w_ref, dtab_ref, rtab_ref, o_ref):
    x = x_ref[...]
    h = (x * lax.rsqrt(jnp.mean(x * x, axis=-1, keepdims=True) + EPS) * g_ref[...]).astype(BF16)
    chunk = 512
    for c in range(IN_COLS // chunk):
        p = jnp.dot(h, w_ref[:, c * chunk:(c + 1) * chunk], preferred_element_type=F32)
        for j in range(chunk // LANES):
            slab = c * (chunk // LANES) + j
            ps = p[:, j * LANES:(j + 1) * LANES]
            if slab < 8:
                ps = _rotate(ps, dtab_ref, DIFF_QK_DIM // ROPE_FRACTION // 2)
                if slab < 4:
                    ps = ps * (DIFF_QK_DIM ** -0.5 * math.log2(math.e))
            elif 12 <= slab < 16:
                ps = _rotate(ps, rtab_ref, RET_QK_DIM // 2)
                if slab >= 14:
                    ps = ps * (RET_QK_DIM ** -0.5)
            o_ref[:, slab * LANES:(slab + 1) * LANES] = ps.astype(BF16)


def _inproj(x2d, g, w_bf16, dtab, rtab, seq, tm):
    tokens = x2d.shape[0]
    nseq = seq // tm
    return pl.pallas_call(
        _inproj_kernel,
        grid=(tokens // tm,),
        in_specs=[
            pl.BlockSpec((tm, D_MODEL), lambda i: (i, 0)),
            pl.BlockSpec((1, D_MODEL), lambda i: (0, 0)),
            pl.BlockSpec((D_MODEL, IN_COLS), lambda i: (0, 0)),
            pl.BlockSpec((3, tm, LANES), lambda i: (0, i % nseq, 0)),
            pl.BlockSpec((3, tm, LANES), lambda i: (0, i % nseq, 0)),
        ],
        out_specs=pl.BlockSpec((tm, IN_COLS), lambda i: (i, 0)),
        out_shape=jax.ShapeDtypeStruct((tokens, IN_COLS), BF16),
        compiler_params=_compiler_params(("parallel",)),
    )(x2d, g, w_bf16, dtab, rtab)


def _diffattn_kernel(lam_ref, q_ref, k_ref, v_ref, g_ref, o_ref):
    q = q_ref[0]
    k = k_ref[0]
    v = v_ref[0]
    lane = lax.broadcasted_iota(jnp.int32, q.shape, 1)
    zero = jnp.zeros_like(q)
    q1 = jnp.where(lane < DIFF_QK_DIM, q, zero)
    q2 = jnp.where(lane >= DIFF_QK_DIM, q, zero)

    v_ones = jnp.concatenate([v, jnp.ones_like(v)], axis=1)
    seq = k.shape[0]
    kv_chunk = min(seq, ATTN_KV_CHUNK)

    queries = (q1, q2)
    ms = [jnp.full((q.shape[0], 1), -jnp.inf, F32) for _ in queries]
    accs = [jnp.zeros((q.shape[0], 2 * LANES), F32) for _ in queries]
    for c in range(seq // kv_chunk):
        rows = slice(c * kv_chunk, (c + 1) * kv_chunk)
        for i, qm in enumerate(queries):
            s = lax.dot_general(qm, k[rows], NT_DIMS, preferred_element_type=F32)
            m_new = jnp.maximum(ms[i], jnp.max(s, axis=-1, keepdims=True))
            p = jnp.exp2(s - m_new).astype(BF16)
            accs[i] = (accs[i] * jnp.exp2(ms[i] - m_new)
                       + jnp.dot(p, v_ones[rows], preferred_element_type=F32))
            ms[i] = m_new
    outs = [acc[:, :LANES] / acc[:, LANES:] for acc in accs]

    o = outs[0] - lam_ref[0] * outs[1]
    o = o * lax.rsqrt(jnp.mean(o * o, axis=-1, keepdims=True) + EPS) * g_ref[...]
    o_ref[0] = (o * (1.0 - LAMBDA_INIT)).astype(BF16)


def _diffattn(lam, proj3, g, tq):
    batch, seq, _ = proj3.shape
    return pl.pallas_call(
        _diffattn_kernel,
        grid=(batch, N_DIFF_HEADS, seq // tq),
        in_specs=[
            pl.BlockSpec(memory_space=pltpu.SMEM),
            pl.BlockSpec((1, tq, LANES), lambda b, h, i: (b, i, h)),
            pl.BlockSpec((1, seq, LANES), lambda b, h, i: (b, 0, 4 + h)),
            pl.BlockSpec((1, seq, LANES), lambda b, h, i: (b, 0, 8 + h)),
            pl.BlockSpec((1, LANES), lambda b, h, i: (0, 0)),
        ],
        out_specs=pl.BlockSpec((1, tq, LANES), lambda b, h, i: (b, i, h)),
        out_shape=jax.ShapeDtypeStruct((batch, seq, N_DIFF_HEADS * LANES), BF16),
        compiler_params=_compiler_params(("parallel", "parallel", "parallel")),
    )(lam, proj3, proj3, proj3, g)


def _retention_kernel(lg_ref, q_ref, k_ref, v_ref, rg_ref, gn_ref, o_ref, acc_ref, *, chunk):
    hp = pl.program_id(1)
    seq = q_ref.shape[1]
    nchunks = seq // chunk
    lane = lax.broadcasted_iota(jnp.int32, (chunk, LANES), 1)
    ri = lax.broadcasted_iota(jnp.int32, (chunk, chunk), 0).astype(F32)
    ci = lax.broadcasted_iota(jnp.int32, (chunk, chunk), 1).astype(F32)
    pos = lax.broadcasted_iota(jnp.int32, (chunk, 1), 0).astype(F32)

    streams = []
    for hh in range(2):
        hmask = (lane >= RET_QK_DIM * hh) & (lane < RET_QK_DIM * (hh + 1))
        col = slice(hh * LANES, (hh + 1) * LANES)
        for direction in range(2):
            lg = lg_ref[direction, 2 * hp + hh]
            if direction == 0:
                dist = ri - ci
                valid = dist >= 0.0
                q_decay = jnp.exp(lg * (pos + 1.0))
                k_decay = jnp.exp(lg * (chunk - 1.0 - pos))
            else:
                dist = ci - ri
                valid = dist > 0.0
                q_decay = jnp.exp(lg * (chunk - pos))
                k_decay = jnp.exp(lg * pos)
            decay = jnp.where(valid, jnp.exp(lg * jnp.maximum(dist, 0.0)), 0.0)
            chunk_decay = jnp.exp(jnp.full((1, 1), lg * chunk, F32))
            streams.append((direction, hmask, col, decay, q_decay, k_decay, chunk_decay))

    acc_ref[...] = jnp.zeros_like(acc_ref)

    def body(n, states):
        new_states = []
        for state, (direction, hmask, col, decay, q_decay, k_decay, chunk_decay) in zip(states, streams):
            c = n if direction == 0 else nchunks - 1 - n
            rows = pl.ds(pl.multiple_of(c * chunk, chunk), chunk)
            qc = q_ref[0, rows, :]
            kc = k_ref[0, rows, :]
            vc = v_ref[0, rows, col]
            qh = jnp.where(hmask, qc, jnp.zeros_like(qc))
            inner = lax.dot_general(qh, kc, NT_DIMS, preferred_element_type=F32) * decay
            o = jnp.dot(inner.astype(BF16), vc, preferred_element_type=F32)
            o = o + jnp.dot((qh.astype(F32) * q_decay).astype(BF16), state.astype(BF16),
                            preferred_element_type=F32)
            kd = (kc.astype(F32) * k_decay).astype(BF16)
            new_states.append(state * chunk_decay
                              + lax.dot_general(kd, vc, TN_DIMS, preferred_element_type=F32))
            acc_ref[rows, col] += o
        return tuple(new_states)

    lax.fori_loop(0, nchunks, body, tuple(jnp.zeros((LANES, LANES), F32) for _ in streams))

    for hh in range(2):
        col = slice(hh * LANES, (hh + 1) * LANES)
        o = acc_ref[:, col]
        cen = o - jnp.mean(o, axis=-1, keepdims=True)
        var = jnp.mean(cen * cen, axis=-1, keepdims=True)
        y = cen * lax.rsqrt(var + EPS) * gn_ref[:, col]
        gate = rg_ref[0, :, col].astype(F32)
        o_ref[0, :, col] = (y * gate * jax.nn.sigmoid(gate)).astype(BF16)


def _retention(log_gamma, proj3, gn, chunk):
    batch, seq, _ = proj3.shape
    wide = 2 * LANES
    return pl.pallas_call(
        functools.partial(_retention_kernel, chunk=chunk),
        grid=(batch, N_RET_HEADS // 2),
        in_specs=[
            pl.BlockSpec(memory_space=pltpu.SMEM),
            pl.BlockSpec((1, seq, LANES), lambda b, h: (b, 0, 12 + h)),
            pl.BlockSpec((1, seq, LANES), lambda b, h: (b, 0, 14 + h)),
            pl.BlockSpec((1, seq, wide), lambda b, h: (b, 0, 8 + h)),
            pl.BlockSpec((1, seq, wide), lambda b, h: (b, 0, 10 + h)),
            pl.BlockSpec((1, wide), lambda b, h: (0, h)),
        ],
        out_specs=pl.BlockSpec((1, seq, wide), lambda b, h: (b, 0, h)),
        out_shape=jax.ShapeDtypeStruct((batch, seq, N_RET_HEADS * LANES), BF16),
        scratch_shapes=[pltpu.VMEM((seq, wide), F32)],
        compiler_params=_compiler_params(("parallel", "parallel")),
    )(log_gamma, proj3, proj3, proj3, proj3, gn)


def _outproj_kernel(a_ref, r_ref, x_ref, wo_ref, fg_ref, wq_ref, keys_ref, x1_ref, xn_ref, st_ref):
    half = wo_ref.shape[0] // 2
    y = x_ref[...]
    y = y + jnp.dot(a_ref[...], wo_ref[0:half, :], preferred_element_type=F32)
    y = y + jnp.dot(r_ref[...], wo_ref[half:2 * half, :], preferred_element_type=F32)
    x1_ref[...] = y
    xn = (y * lax.rsqrt(jnp.mean(y * y, axis=-1, keepdims=True) + EPS) * fg_ref[...]).astype(BF16)
    xn_ref[...] = xn
    q = jnp.dot(xn, wq_ref[...], preferred_element_type=F32).astype(BF16)
    for hp in range(2 * PEER_HEADS):
        st_ref[hp] = lax.dot_general(keys_ref[hp], q[:, hp * LANES:(hp + 1) * LANES], NT_DIMS,
                                     preferred_element_type=F32)


def _outproj(a2d, r2d, x2d, wo, fg, wq, keys, tm):
    tokens = x2d.shape[0]
    nsets = 2 * PEER_HEADS
    return pl.pallas_call(
        _outproj_kernel,
        grid=(tokens // tm,),
        in_specs=[
            pl.BlockSpec((tm, D_MODEL // 2), lambda i: (i, 0)),
            pl.BlockSpec((tm, D_MODEL // 2), lambda i: (i, 0)),
            pl.BlockSpec((tm, D_MODEL), lambda i: (i, 0)),
            pl.BlockSpec((D_MODEL, D_MODEL), lambda i: (0, 0)),
            pl.BlockSpec((1, D_MODEL), lambda i: (0, 0)),
            pl.BlockSpec((D_MODEL, nsets * LANES), lambda i: (0, 0)),
            pl.BlockSpec((nsets, N_KEYS, LANES), lambda i: (0, 0, 0)),
        ],
        out_specs=[
            pl.BlockSpec((tm, D_MODEL), lambda i: (i, 0)),
            pl.BlockSpec((tm, D_MODEL), lambda i: (i, 0)),
            pl.BlockSpec((nsets, N_KEYS, tm), lambda i: (0, 0, i)),
        ],
        out_shape=[
            jax.ShapeDtypeStruct((tokens, D_MODEL), F32),
            jax.ShapeDtypeStruct((tokens, D_MODEL), BF16),
            jax.ShapeDtypeStruct((nsets, N_KEYS, tokens), F32),
        ],
        compiler_params=_compiler_params(("parallel",)),
    )(a2d, r2d, x2d, wo, fg, wq, keys)


def _extract_top(s, code, big, count):
    vals, codes = [], []
    for _ in range(count):
        m = jnp.max(s, axis=0, keepdims=True)
        sel = jnp.min(jnp.where(s == m, code, big), axis=0, keepdims=True)
        vals.append(m)
        codes.append(sel)
        s = jnp.where(code == sel, -jnp.inf, s)
    return jnp.concatenate(vals, axis=0), jnp.concatenate(codes, axis=0)


def _ordered(a, b):
    (va, ca), (vb, cb) = a, b
    a_first = (va > vb) | ((va == vb) & (ca < cb))
    return ((jnp.where(a_first, va, vb), jnp.where(a_first, ca, cb)),
            (jnp.where(a_first, vb, va), jnp.where(a_first, cb, ca)))


def _extract_top_keys(s, row8, count):
    depth, n_stacks = 4, N_KEYS // SUBLANES // 4
    big = float(N_KEYS)
    levels = [[None] * n_stacks for _ in range(depth)]
    for st in range(n_stacks):
        e = [(s[(st * depth + d) * SUBLANES:(st * depth + d + 1) * SUBLANES, :],
              row8 + float((st * depth + d) * SUBLANES)) for d in range(depth)]
        e[0], e[1] = _ordered(e[0], e[1])
        e[2], e[3] = _ordered(e[2], e[3])
        e[0], e[2] = _ordered(e[0], e[2])
        e[1], e[3] = _ordered(e[1], e[3])
        e[1], e[2] = _ordered(e[1], e[2])
        for d in range(depth):
            levels[d][st] = e[d]
    vals, codes = [], []
    for _ in range(count):
        tops = levels[0]
        best = jnp.maximum(jnp.maximum(tops[0][0], tops[1][0]), jnp.maximum(tops[2][0], tops[3][0]))
        m = jnp.max(best, axis=0, keepdims=True)
        tied = [jnp.where(v == m, c, big) for v, c in tops]
        sel = jnp.min(jnp.minimum(jnp.minimum(tied[0], tied[1]), jnp.minimum(tied[2], tied[3])),
                      axis=0, keepdims=True)
        vals.append(m)
        codes.append(sel)
        for st in range(n_stacks):
            hit = levels[0][st][1] == sel
            for d in range(depth - 1):
                (v_up, c_up), (v_dn, c_dn) = levels[d][st], levels[d + 1][st]
                levels[d][st] = (jnp.where(hit, v_dn, v_up), jnp.where(hit, c_dn, c_up))
            v_last, c_last = levels[depth - 1][st]
            levels[depth - 1][st] = (jnp.where(hit, -jnp.inf, v_last), jnp.where(hit, big, c_last))
    return jnp.concatenate(vals, axis=0), jnp.concatenate(codes, axis=0)


def _pick_rows(table, sel):
    out = jnp.zeros_like(sel)
    for a in range(PEER_TOPK):
        out = jnp.where(sel == float(a), table[a:a + 1, :], out)
    return out


def _head_picks(s1, s2):
    k = PEER_TOPK
    tb = s1.shape[1]
    row8 = lax.broadcasted_iota(jnp.int32, (SUBLANES, tb), 0).astype(F32)
    neg = jnp.full((SUBLANES, tb), -jnp.inf, F32)
    v1, i1 = _extract_top_keys(s1, row8, k)
    v2, i2 = _extract_top_keys(s2, row8, k)
    cands, codes = [], []
    for a in range(k // 2):
        nb = k // (a + 1)
        for b0 in range(0, nb, SUBLANES):
            c = v1[a:a + 1, :] + v2[b0:b0 + SUBLANES, :]
            if nb - b0 < SUBLANES:
                c = jnp.where(row8 < float(nb - b0), c, neg)
            cands.append(c)
            codes.append(row8 + float(a * k + b0))
    cands.append(v1[k // 2:k, :] + v2[0:1, :])
    codes.append((row8 + float(k // 2)) * float(k))
    best, code = _extract_top(jnp.concatenate(cands, axis=0), jnp.concatenate(codes, axis=0),
                              float(k * k), k)
    a_sel = jnp.floor(code * (1.0 / k))
    b_sel = code - a_sel * k
    e = jnp.exp(best - best[0:1, :])
    gate = e / jnp.sum(e, axis=0, keepdims=True)
    return _pick_rows(i1, a_sel), _pick_rows(i2, b_sel), gate


def _topk_kernel(st_ref, i_ref, j_ref, g_ref, i_scr, j_scr, g_scr):
    k = PEER_TOPK

    def head_body(h, carry):
        rows = pl.ds(pl.multiple_of(h * k, k), k)
        i_scr[rows, :], j_scr[rows, :], g_scr[rows, :] = _head_picks(st_ref[2 * h], st_ref[2 * h + 1])
        return carry

    lax.fori_loop(0, PEER_HEADS, head_body, 0, unroll=4)
    i_ref[...] = i_scr[...].T
    j_ref[...] = j_scr[...].T
    g_ref[...] = g_scr[...].T


def _topk(st, tb, tokens):
    nsets = st.shape[0]
    slots = PEER_HEADS * PEER_TOPK
    spec = pl.BlockSpec((tb, slots), lambda i: (i, 0))
    shape = jax.ShapeDtypeStruct((tokens, slots), F32)
    return pl.pallas_call(
        _topk_kernel,
        grid=(tokens // tb,),
        in_specs=[pl.BlockSpec((nsets, N_KEYS, tb), lambda i: (0, 0, i))],
        out_specs=[spec, spec, spec],
        out_shape=[shape, shape, shape],
        scratch_shapes=[pltpu.VMEM((slots, tb), F32)] * 3,
        compiler_params=_compiler_params(("parallel",)),
    )(st)


def _peer_kernel(xn_ref, i_ref, j_ref, g_ref, st_ref, u_ref, v_ref, x1_ref, fg_ref, o_ref,
                 w_scr, gated_scr, sel_scr, picks_scr, *, row_stride, n_tiles, heads_per_step):
    t = pl.program_id(0)
    e = pl.program_id(1)
    bt = xn_ref.shape[0]
    tile = u_ref.shape[0]
    steps_per_lane_block = PEER_HEADS // heads_per_step

    @pl.when(e == 0)
    def _build_weights():
        @pl.when(t == 0)
        def _first_block_picks():
            picks_scr[0] = i_ref[...]
            picks_scr[1] = j_ref[...]
            picks_scr[2] = g_ref[...]

        o_ref[...] = jnp.zeros_like(o_ref)
        gated_scr[...] = jnp.zeros_like(gated_scr)
        sub = lax.broadcasted_iota(jnp.int32, (N_KEYS, LANES), 0).astype(F32)
        zeros = jnp.zeros((N_KEYS, LANES), BF16)

        def onehots(tok):
            irow = picks_scr[0, pl.ds(tok, 1), :]
            jrow = picks_scr[1, pl.ds(tok, 1), :]
            grow = picks_scr[2, pl.ds(tok, 1), :] * 0.5
            pt = jnp.where(sub == irow, 1.0, 0.0).astype(BF16)
            qt = jnp.where(sub == jrow, grow, 0.0).astype(BF16)
            return pt, qt

        def pair_body(p, carry):
            pt_a, qt_a = onehots(p)
            pt_b, qt_b = onehots(p + bt // 2)
            lhs = jnp.concatenate([pt_a, pt_b], axis=1)
            rhs = jnp.concatenate([jnp.concatenate([qt_a, zeros], axis=1),
                                   jnp.concatenate([zeros, qt_b], axis=1)], axis=0)
            w2 = lax.dot_general(lhs, rhs, NT_DIMS, preferred_element_type=F32)
            w_scr[pl.ds(p, N_KEYS, stride=row_stride), :] = w2[:, :LANES]
            w_scr[pl.ds(p + bt // 2, N_KEYS, stride=row_stride), :] = w2[:, LANES:]
            return carry

        lax.fori_loop(0, bt // 2, pair_body, 0, unroll=128)

    def second_matmul():
        o_ref[...] += jnp.dot(gated_scr[...], v_ref[...], preferred_element_type=F32)

    @pl.when(e < n_tiles)
    def _sweep_step():
        lane_block = e // steps_per_lane_block
        head0 = (e % steps_per_lane_block) * heads_per_step
        for hh in range(heads_per_step):
            picks = _head_picks(st_ref[2 * hh], st_ref[2 * hh + 1])
            rows = pl.ds(pl.multiple_of((head0 + hh) * PEER_TOPK, PEER_TOPK), PEER_TOPK)
            for c in range(3):
                sel_scr[lane_block, c, rows, :] = picks[c]

        second_matmul()

        act = lax.dot_general(xn_ref[...], u_ref[...], NT_DIMS, preferred_element_type=F32)
        i0 = e * (tile // N_KEYS)
        w = jnp.concatenate(
            [w_scr[pl.ds(pl.multiple_of((i0 + ii) * row_stride, SUBLANES), bt), :]
             for ii in range(tile // N_KEYS)], axis=1)
        gated = w * (act * (1.0 + lax.erf(act * (2.0 ** -0.5))))
        gated_scr[...] = gated.astype(BF16)

    @pl.when(e == n_tiles)
    def _finish():
        second_matmul()
        y = x1_ref[...] + o_ref[...]
        o_ref[...] = y * lax.rsqrt(jnp.mean(y * y, axis=-1, keepdims=True) + EPS) * fg_ref[...]
        for lb in range(bt // LANES):
            for c in range(3):
                picks_scr[c, lb * LANES:(lb + 1) * LANES, :] = sel_scr[lb, c].T


def _peer(xn, i_first, j_first, g_first, scores_t, u_bf16, v_bf16, x1, fg, bt, tile):
    tokens = xn.shape[0]
    n_blocks = tokens // bt
    n_tiles = u_bf16.shape[0] // tile
    lane_blocks = bt // LANES
    steps_per_lane_block = n_tiles // lane_blocks
    heads_per_step = PEER_HEADS // steps_per_lane_block
    assert (n_tiles == lane_blocks * steps_per_lane_block
            and PEER_HEADS == heads_per_step * steps_per_lane_block)
    row_stride = bt + SUBLANES
    slots = PEER_HEADS * PEER_TOPK
    tok = lambda width: pl.BlockSpec((bt, width), lambda t, e: (t, 0))
    first = pl.BlockSpec((bt, slots), lambda t, e: (0, 0))

    def scores_index(t, e):
        ec = jnp.minimum(e, n_tiles - 1)
        nxt = jnp.minimum(t + 1, n_blocks - 1)
        return (ec % steps_per_lane_block, 0, nxt * lane_blocks + ec // steps_per_lane_block)

    return pl.pallas_call(
        functools.partial(_peer_kernel, row_stride=row_stride, n_tiles=n_tiles,
                          heads_per_step=heads_per_step),
        grid=(n_blocks, n_tiles + 1),
        in_specs=[
            tok(D_MODEL), first, first, first,
            pl.BlockSpec((2 * heads_per_step, N_KEYS, LANES), scores_index),
            pl.BlockSpec((tile, D_MODEL), lambda t, e: (jnp.minimum(e, n_tiles - 1), 0)),
            pl.BlockSpec((tile, D_MODEL), lambda t, e: (jnp.maximum(e - 1, 0), 0)),
            tok(D_MODEL),
            pl.BlockSpec((1, D_MODEL), lambda t, e: (0, 0)),
        ],
        out_specs=tok(D_MODEL),
        out_shape=jax.ShapeDtypeStruct((tokens, D_MODEL), F32),
        scratch_shapes=[pltpu.VMEM((N_KEYS * row_stride, LANES), F32),
                        pltpu.VMEM((bt, tile), BF16),
                        pltpu.VMEM((lane_blocks, 3, slots, LANES), F32),
                        pltpu.VMEM((3, bt, slots), F32)],
        compiler_params=_compiler_params(("arbitrary", "arbitrary"), VMEM_LIMIT_PEER),
    )(xn, i_first, j_first, g_first, scores_t, u_bf16, v_bf16, x1, fg)


def kernel(x, attn_norm_g, w_in, diff_lambda, diff_norm_g, ret_log_decay, ret_norm_g, w_out,
           ffn_norm_g, peer_w_query, peer_sub_keys, peer_u, peer_v, final_norm_g):
    batch, seq, d_model = x.shape
    assert d_model == D_MODEL and seq % LANES == 0
    tokens = batch * seq
    tm = min(ROW_BLOCK, seq)
    x2d = x.reshape(tokens, D_MODEL)

    rot_dim = DIFF_QK_DIM // ROPE_FRACTION
    rope_inv = jnp.power(jnp.float32(ROPE_THETA), -jnp.arange(rot_dim // 2, dtype=F32) * 2.0 / rot_dim)
    ret_inv = 1.0 / jnp.power(jnp.float32(RET_THETA), jnp.linspace(0.0, 1.0, RET_QK_DIM // 2, dtype=F32))
    dtab = _rotary_tables(seq, rope_inv, rot_dim // 2)
    rtab = _rotary_tables(seq, ret_inv, RET_QK_DIM // 2)

    proj = _inproj(x2d, attn_norm_g[0].reshape(1, D_MODEL), w_in[0].astype(BF16), dtab, rtab, seq, tm)
    proj3 = proj.reshape(batch, seq, IN_COLS)

    lam_p = diff_lambda[0].astype(F32)
    lam = (jnp.exp(jnp.sum(lam_p[0] * lam_p[1])) - jnp.exp(jnp.sum(lam_p[2] * lam_p[3]))
           + LAMBDA_INIT).reshape(1).astype(F32)
    attn = _diffattn(lam, proj3, diff_norm_g[0].reshape(1, LANES),
                     min(ATTN_Q_BLOCK, seq))

    log_gamma = -jnp.exp(ret_log_decay[0].astype(F32))
    ret = _retention(log_gamma, proj3, ret_norm_g[0].reshape(1, N_RET_HEADS * LANES),
                     min(RET_CHUNK, seq))

    x1, xn, scores_t = _outproj(
        attn.reshape(tokens, D_MODEL // 2), ret.reshape(tokens, D_MODEL // 2), x2d,
        w_out[0].astype(BF16), ffn_norm_g[0].reshape(1, D_MODEL), peer_w_query[0].astype(BF16),
        peer_sub_keys[0].reshape(2 * PEER_HEADS, N_KEYS, LANES).astype(BF16), tm)

    bt = min(PEER_TOKEN_BLOCK, tokens)
    i_first, j_first, g_first = _topk(scores_t, TOPK_LANE_BLOCK, bt)

    y = _peer(xn, i_first, j_first, g_first, scores_t, peer_u[0].astype(BF16),
              peer_v[0].astype(BF16), x1, final_norm_g.reshape(1, D_MODEL), bt, PEER_EXPERT_TILE)
    return y.reshape(batch, seq, D_MODEL)
```

```python
import functools
import math

import jax
import jax.numpy as jnp
from jax import lax
from jax.experimental import pallas as pl
from jax.experimental.pallas import tpu as pltpu

F32 = jnp.float32
BF16 = jnp.bfloat16

D_MODEL = 1024
N_DIFF_HEADS = 4
DIFF_QK_DIM = 64
N_RET_HEADS = 4
RET_QK_DIM = 64
IN_COLS = 3072
ROPE_THETA = 500000.0
ROPE_FRACTION = 4
RET_THETA = 10000.0
N_KEYS = 128
PEER_HEADS = 8
PEER_TOPK = 16
EPS = 1e-6
LAMBDA_INIT = 0.8 - 0.6 * math.exp(0.0)

LANES = 128
SUBLANES = 8
MXU_DIM = 256
VMEM_BYTES_V7X = 64 * 1024 * 1024

ROW_BLOCK = 512
ATTN_Q_BLOCK = 1024
ATTN_KV_CHUNK = 256
RET_CHUNK = 256
TOPK_LANE_BLOCK = LANES
PEER_TOKEN_BLOCK = 512
PEER_EXPERT_TILE = 1024

VMEM_LIMIT_PEER = VMEM_BYTES_V7X - 4 * 1024 * 1024
VMEM_LIMIT_DEFAULT = VMEM_BYTES_V7X * 3 // 4

NT_DIMS = (((1,), (1,)), ((), ()))
TN_DIMS = (((0,), (0,)), ((), ()))


def _dot_by_depth(a, b, transpose_b=False):
    depth = a.shape[1]
    parts = []
    for k0 in range(0, depth, MXU_DIM):
        ks = slice(k0, min(k0 + MXU_DIM, depth))
        if transpose_b:
            parts.append(lax.dot_general(a[:, ks], b[:, ks], NT_DIMS, preferred_element_type=F32))
        else:
            parts.append(jnp.dot(a[:, ks], b[ks, :], preferred_element_type=F32))
    return sum(parts)


def _compiler_params(semantics, vmem_bytes=VMEM_LIMIT_DEFAULT):
    return pltpu.CompilerParams(dimension_semantics=semantics, vmem_limit_bytes=vmem_bytes)


def _rotary_tables(seq, inv_freq, half):
    pos = jnp.arange(seq, dtype=F32)
    lane = jnp.arange(LANES)
    lp = lane % 64
    rotated = lp < 2 * half
    freq = jnp.where(rotated, lp % half, 0)
    ang = pos[:, None] * inv_freq[freq][None, :]
    cos = jnp.where(rotated[None, :], jnp.cos(ang), 1.0)
    sin = jnp.sin(ang)
    sin_lo = jnp.where((lp < half)[None, :], -sin, 0.0)
    sin_hi = jnp.where((rotated & (lp >= half))[None, :], sin, 0.0)
    return jnp.stack([cos, sin_lo, sin_hi]).astype(F32)


def _rotate(p, tab_ref, half):
    up = pltpu.roll(p, LANES - half, axis=1)
    dn = pltpu.roll(p, half, axis=1)
    return p * tab_ref[0] + up * tab_ref[1] + dn * tab_ref[2]


def _inproj_kernel(x_ref, g_ref, w_ref, dtab_ref, rtab_ref, o_ref):
    x = x_ref[...]
    h = (x * lax.rsqrt(jnp.mean(x * x, axis=-1, keepdims=True) + EPS) * g_ref[...]).astype(BF16)
    chunk = 512
    for c in range(IN_COLS // chunk):
        p = jnp.dot(h, w_ref[:, c * chunk:(c + 1) * chunk], preferred_element_type=F32)
        for j in range(chunk // LANES):
            slab = c * (chunk // LANES) + j
            ps = p[:, j * LANES:(j + 1) * LANES]
            if slab < 8:
                ps = _rotate(ps, dtab_ref, DIFF_QK_DIM // ROPE_FRACTION // 2)
                if slab < 4:
                    ps = ps * (DIFF_QK_DIM ** -0.5 * math.log2(math.e))
            elif 12 <= slab < 16:
                ps = _rotate(ps, rtab_ref, RET_QK_DIM // 2)
                if slab >= 14:
                    ps = ps * (RET_QK_DIM ** -0.5)
            o_ref[:, slab * LANES:(slab + 1) * LANES] = ps.astype(BF16)


def _inproj(x2d, g, w_bf16, dtab, rtab, seq, tm):
    tokens = x2d.shape[0]
    nseq = seq // tm
    return pl.pallas_call(
        _inproj_kernel,
        grid=(tokens // tm,),
        in_specs=[
            pl.BlockSpec((tm, D_MODEL), lambda i: (i, 0)),
            pl.BlockSpec((1, D_MODEL), lambda i: (0, 0)),
            pl.BlockSpec((D_MODEL, IN_COLS), lambda i: (0, 0)),
            pl.BlockSpec((3, tm, LANES), lambda i: (0, i % nseq, 0)),
            pl.BlockSpec((3, tm, LANES), lambda i: (0, i % nseq, 0)),
        ],
        out_specs=pl.BlockSpec((tm, IN_COLS), lambda i: (i, 0)),
        out_shape=jax.ShapeDtypeStruct((tokens, IN_COLS), BF16),
        compiler_params=_compiler_params(("parallel",)),
    )(x2d, g, w_bf16, dtab, rtab)


def _diffattn_kernel(lam_ref, q_ref, k_ref, v_ref, g_ref, o_ref):
    q = q_ref[0]
    k = k_ref[0]
    v = v_ref[0]
    lane = lax.broadcasted_iota(jnp.int32, q.shape, 1)
    zero = jnp.zeros_like(q)
    q1 = jnp.where(lane < DIFF_QK_DIM, q, zero)
    q2 = jnp.where(lane >= DIFF_QK_DIM, q, zero)

    v_ones = jnp.concatenate([v, jnp.ones_like(v)], axis=1)
    seq = k.shape[0]
    kv_chunk = min(seq, ATTN_KV_CHUNK)

    queries = (q1, q2)
    ms = [jnp.full((q.shape[0], 1), -jnp.inf, F32) for _ in queries]
    accs = [jnp.zeros((q.shape[0], 2 * LANES), F32) for _ in queries]
    for c in range(seq // kv_chunk):
        rows = slice(c * kv_chunk, (c + 1) * kv_chunk)
        for i, qm in enumerate(queries):
            s = lax.dot_general(qm, k[rows], NT_DIMS, preferred_element_type=F32)
            m_new = jnp.maximum(ms[i], jnp.max(s, axis=-1, keepdims=True))
            p = jnp.exp2(s - m_new).astype(BF16)
            accs[i] = (accs[i] * jnp.exp2(ms[i] - m_new)
                       + jnp.dot(p, v_ones[rows], preferred_element_type=F32))
            ms[i] = m_new
    outs = [acc[:, :LANES] / acc[:, LANES:] for acc in accs]

    o = outs[0] - lam_ref[0] * outs[1]
    o = o * lax.rsqrt(jnp.mean(o * o, axis=-1, keepdims=True) + EPS) * g_ref[...]
    o_ref[0] = (o * (1.0 - LAMBDA_INIT)).astype(BF16)


def _diffattn(lam, proj3, g, tq):
    batch, seq, _ = proj3.shape
    return pl.pallas_call(
        _diffattn_kernel,
        grid=(batch, N_DIFF_HEADS, seq // tq),
        in_specs=[
            pl.BlockSpec(memory_space=pltpu.SMEM),
            pl.BlockSpec((1, tq, LANES), lambda b, h, i: (b, i, h)),
            pl.BlockSpec((1, seq, LANES), lambda b, h, i: (b, 0, 4 + h)),
            pl.BlockSpec((1, seq, LANES), lambda b, h, i: (b, 0, 8 + h)),
            pl.BlockSpec((1, LANES), lambda b, h, i: (0, 0)),
        ],
        out_specs=pl.BlockSpec((1, tq, LANES), lambda b, h, i: (b, i, h)),
        out_shape=jax.ShapeDtypeStruct((batch, seq, N_DIFF_HEADS * LANES), BF16),
        compiler_params=_compiler_params(("parallel", "parallel", "parallel")),
    )(lam, proj3, proj3, proj3, g)


def _retention_kernel(lg_ref, q_ref, k_ref, v_ref, rg_ref, gn_ref, o_ref, acc_ref, *, chunk):
    hp = pl.program_id(1)
    seq = q_ref.shape[1]
    nchunks = seq // chunk
    lane = lax.broadcasted_iota(jnp.int32, (chunk, LANES), 1)
    ri = lax.broadcasted_iota(jnp.int32, (chunk, chunk), 0).astype(F32)
    ci = lax.broadcasted_iota(jnp.int32, (chunk, chunk), 1).astype(F32)
    pos = lax.broadcasted_iota(jnp.int32, (chunk, 1), 0).astype(F32)

    streams = []
    for hh in range(2):
        hmask = (lane >= RET_QK_DIM * hh) & (lane < RET_QK_DIM * (hh + 1))
        col = slice(hh * LANES, (hh + 1) * LANES)
        for direction in range(2):
            lg = lg_ref[direction, 2 * hp + hh]
            if direction == 0:
                dist = ri - ci
                valid = dist >= 0.0
                q_decay = jnp.exp(lg * (pos + 1.0))
                k_decay = jnp.exp(lg * (chunk - 1.0 - pos))
            else:
                dist = ci - ri
                valid = dist > 0.0
                q_decay = jnp.exp(lg * (chunk - pos))
                k_decay = jnp.exp(lg * pos)
            decay = jnp.where(valid, jnp.exp(lg * jnp.maximum(dist, 0.0)), 0.0)
            chunk_decay = jnp.exp(jnp.full((1, 1), lg * chunk, F32))
            streams.append((direction, hmask, col, decay, q_decay, k_decay, chunk_decay))

    acc_ref[...] = jnp.zeros_like(acc_ref)

    def body(n, states):
        new_states = []
        for state, (direction, hmask, col, decay, q_decay, k_decay, chunk_decay) in zip(states, streams):
            c = n if direction == 0 else nchunks - 1 - n
            rows = pl.ds(pl.multiple_of(c * chunk, chunk), chunk)
            qc = q_ref[0, rows, :]
            kc = k_ref[0, rows, :]
            vc = v_ref[0, rows, col]
            qh = jnp.where(hmask, qc, jnp.zeros_like(qc))
            inner = lax.dot_general(qh, kc, NT_DIMS, preferred_element_type=F32) * decay
            o = jnp.dot(inner.astype(BF16), vc, preferred_element_type=F32)
            o = o + jnp.dot((qh.astype(F32) * q_decay).astype(BF16), state.astype(BF16),
                            preferred_element_type=F32)
            kd = (kc.astype(F32) * k_decay).astype(BF16)
            new_states.append(state * chunk_decay
                              + lax.dot_general(kd, vc, TN_DIMS, preferred_element_type=F32))
            acc_ref[rows, col] += o
        return tuple(new_states)

    lax.fori_loop(0, nchunks, body, tuple(jnp.zeros((LANES, LANES), F32) for _ in streams))

    for hh in range(2):
        col = slice(hh * LANES, (hh + 1) * LANES)
        o = acc_ref[:, col]
        cen = o - jnp.mean(o, axis=-1, keepdims=True)
        var = jnp.mean(cen * cen, axis=-1, keepdims=True)
        y = cen * lax.rsqrt(var + EPS) * gn_ref[:, col]
        gate = rg_ref[0, :, col].astype(F32)
        o_ref[0, :, col] = (y * gate * jax.nn.sigmoid(gate)).astype(BF16)


def _retention(log_gamma, proj3, gn, chunk):
    batch, seq, _ = proj3.shape
    wide = 2 * LANES
    return pl.pallas_call(
        functools.partial(_retention_kernel, chunk=chunk),
        grid=(batch, N_RET_HEADS // 2),
        in_specs=[
            pl.BlockSpec(memory_space=pltpu.SMEM),
            pl.BlockSpec((1, seq, LANES), lambda b, h: (b, 0, 12 + h)),
            pl.BlockSpec((1, seq, LANES), lambda b, h: (b, 0, 14 + h)),
            pl.BlockSpec((1, seq, wide), lambda b, h: (b, 0, 8 + h)),
            pl.BlockSpec((1, seq, wide), lambda b, h: (b, 0, 10 + h)),
            pl.BlockSpec((1, wide), lambda b, h: (0, h)),
        ],
        out_specs=pl.BlockSpec((1, seq, wide), lambda b, h: (b, 0, h)),
        out_shape=jax.ShapeDtypeStruct((batch, seq, N_RET_HEADS * LANES), BF16),
        scratch_shapes=[pltpu.VMEM((seq, wide), F32)],
        compiler_params=_compiler_params(("parallel", "parallel")),
    )(log_gamma, proj3, proj3, proj3, proj3, gn)


def _outproj_kernel(a_ref, r_ref, x_ref, wo_ref, fg_ref, wq_ref, keys_ref, x1_ref, xn_ref, st_ref):
    half = wo_ref.shape[0] // 2
    y = x_ref[...]
    y = y + jnp.dot(a_ref[...], wo_ref[0:half, :], preferred_element_type=F32)
    y = y + jnp.dot(r_ref[...], wo_ref[half:2 * half, :], preferred_element_type=F32)
    x1_ref[...] = y
    xn = (y * lax.rsqrt(jnp.mean(y * y, axis=-1, keepdims=True) + EPS) * fg_ref[...]).astype(BF16)
    xn_ref[...] = xn
    q = jnp.dot(xn, wq_ref[...], preferred_element_type=F32).astype(BF16)
    for hp in range(2 * PEER_HEADS):
        st_ref[hp] = lax.dot_general(keys_ref[hp], q[:, hp * LANES:(hp + 1) * LANES], NT_DIMS,
                                     preferred_element_type=F32)


def _outproj(a2d, r2d, x2d, wo, fg, wq, keys, tm):
    tokens = x2d.shape[0]
    nsets = 2 * PEER_HEADS
    return pl.pallas_call(
        _outproj_kernel,
        grid=(tokens // tm,),
        in_specs=[
            pl.BlockSpec((tm, D_MODEL // 2), lambda i: (i, 0)),
            pl.BlockSpec((tm, D_MODEL // 2), lambda i: (i, 0)),
            pl.BlockSpec((tm, D_MODEL), lambda i: (i, 0)),
            pl.BlockSpec((D_MODEL, D_MODEL), lambda i: (0, 0)),
            pl.BlockSpec((1, D_MODEL), lambda i: (0, 0)),
            pl.BlockSpec((D_MODEL, nsets * LANES), lambda i: (0, 0)),
            pl.BlockSpec((nsets, N_KEYS, LANES), lambda i: (0, 0, 0)),
        ],
        out_specs=[
            pl.BlockSpec((tm, D_MODEL), lambda i: (i, 0)),
            pl.BlockSpec((tm, D_MODEL), lambda i: (i, 0)),
            pl.BlockSpec((nsets, N_KEYS, tm), lambda i: (0, 0, i)),
        ],
        out_shape=[
            jax.ShapeDtypeStruct((tokens, D_MODEL), F32),
            jax.ShapeDtypeStruct((tokens, D_MODEL), BF16),
            jax.ShapeDtypeStruct((nsets, N_KEYS, tokens), F32),
        ],
        compiler_params=_compiler_params(("parallel",)),
    )(a2d, r2d, x2d, wo, fg, wq, keys)


def _extract_top(s, code, big, count):
    vals, codes = [], []
    for _ in range(count):
        m = jnp.max(s, axis=0, keepdims=True)
        sel = jnp.min(jnp.where(s == m, code, big), axis=0, keepdims=True)
        vals.append(m)
        codes.append(sel)
        s = jnp.where(code == sel, -jnp.inf, s)
    return jnp.concatenate(vals, axis=0), jnp.concatenate(codes, axis=0)


def _ordered(a, b):
    (va, ca), (vb, cb) = a, b
    a_first = (va > vb) | ((va == vb) & (ca < cb))
    return ((jnp.where(a_first, va, vb), jnp.where(a_first, ca, cb)),
            (jnp.where(a_first, vb, va), jnp.where(a_first, cb, ca)))


def _extract_top_keys(s, row8, count):
    depth, n_stacks = 4, N_KEYS // SUBLANES // 4
    big = float(N_KEYS)
    levels = [[None] * n_stacks for _ in range(depth)]
    for st in range(n_stacks):
        e = [(s[(st * depth + d) * SUBLANES:(st * depth + d + 1) * SUBLANES, :],
              row8 + float((st * depth + d) * SUBLANES)) for d in range(depth)]
        e[0], e[1] = _ordered(e[0], e[1])
        e[2], e[3] = _ordered(e[2], e[3])
        e[0], e[2] = _ordered(e[0], e[2])
        e[1], e[3] = _ordered(e[1], e[3])
        e[1], e[2] = _ordered(e[1], e[2])
        for d in range(depth):
            levels[d][st] = e[d]
    vals, codes = [], []
    for _ in range(count):
        tops = levels[0]
        best = jnp.maximum(jnp.maximum(tops[0][0], tops[1][0]), jnp.maximum(tops[2][0], tops[3][0]))
        m = jnp.max(best, axis=0, keepdims=True)
        tied = [jnp.where(v == m, c, big) for v, c in tops]
        sel = jnp.min(jnp.minimum(jnp.minimum(tied[0], tied[1]), jnp.minimum(tied[2], tied[3])),
                      axis=0, keepdims=True)
        vals.append(m)
        codes.append(sel)
        for st in range(n_stacks):
            hit = levels[0][st][1] == sel
            for d in range(depth - 1):
                (v_up, c_up), (v_dn, c_dn) = levels[d][st], levels[d + 1][st]
                levels[d][st] = (jnp.where(hit, v_dn, v_up), jnp.where(hit, c_dn, c_up))
            v_last, c_last = levels[depth - 1][st]
            levels[depth - 1][st] = (jnp.where(hit, -jnp.inf, v_last), jnp.where(hit, big, c_last))
    return jnp.concatenate(vals, axis=0), jnp.concatenate(codes, axis=0)


def _pick_rows(table, sel):
    out = jnp.zeros_like(sel)
    for a in range(PEER_TOPK):
        out = jnp.where(sel == float(a), table[a:a + 1, :], out)
    return out


def _head_picks(s1, s2):
    k = PEER_TOPK
    tb = s1.shape[1]
    row8 = lax.broadcasted_iota(jnp.int32, (SUBLANES, tb), 0).astype(F32)
    neg = jnp.full((SUBLANES, tb), -jnp.inf, F32)
    v1, i1 = _extract_top_keys(s1, row8, k)
    v2, i2 = _extract_top_keys(s2, row8, k)
    cands, codes = [], []
    for a in range(k // 2):
        nb = k // (a + 1)
        for b0 in range(0, nb, SUBLANES):
            c = v1[a:a + 1, :] + v2[b0:b0 + SUBLANES, :]
            if nb - b0 < SUBLANES:
                c = jnp.where(row8 < float(nb - b0), c, neg)
            cands.append(c)
            codes.append(row8 + float(a * k + b0))
    cands.append(v1[k // 2:k, :] + v2[0:1, :])
    codes.append((row8 + float(k // 2)) * float(k))
    best, code = _extract_top(jnp.concatenate(cands, axis=0), jnp.concatenate(codes, axis=0),
                              float(k * k), k)
    a_sel = jnp.floor(code * (1.0 / k))
    b_sel = code - a_sel * k
    e = jnp.exp(best - best[0:1, :])
    gate = e / jnp.sum(e, axis=0, keepdims=True)
    return _pick_rows(i1, a_sel), _pick_rows(i2, b_sel), gate


def _topk_kernel(st_ref, i_ref, j_ref, g_ref, i_scr, j_scr, g_scr):
    k = PEER_TOPK

    def head_body(h, carry):
        rows = pl.ds(pl.multiple_of(h * k, k), k)
        i_scr[rows, :], j_scr[rows, :], g_scr[rows, :] = _head_picks(st_ref[2 * h], st_ref[2 * h + 1])
        return carry

    lax.fori_loop(0, PEER_HEADS, head_body, 0, unroll=4)
    i_ref[...] = i_scr[...].T
    j_ref[...] = j_scr[...].T
    g_ref[...] = g_scr[...].T


def _topk(st, tb, tokens):
    nsets = st.shape[0]
    slots = PEER_HEADS * PEER_TOPK
    spec = pl.BlockSpec((tb, slots), lambda i: (i, 0))
    shape = jax.ShapeDtypeStruct((tokens, slots), F32)
    return pl.pallas_call(
        _topk_kernel,
        grid=(tokens // tb,),
        in_specs=[pl.BlockSpec((nsets, N_KEYS, tb), lambda i: (0, 0, i))],
        out_specs=[spec, spec, spec],
        out_shape=[shape, shape, shape],
        scratch_shapes=[pltpu.VMEM((slots, tb), F32)] * 3,
        compiler_params=_compiler_params(("parallel",)),
    )(st)


def _peer_kernel(xn_ref, i_ref, j_ref, g_ref, st_ref, u_ref, v_ref, x1_ref, fg_ref, o_ref,
                 w_scr, gated_scr, sel_scr, picks_scr, *, row_stride, n_tiles, heads_per_step):
    t = pl.program_id(0)
    e = pl.program_id(1)
    bt = xn_ref.shape[0]
    tile = u_ref.shape[0]
    steps_per_lane_block = PEER_HEADS // heads_per_step

    @pl.when(e == 0)
    def _build_weights():
        @pl.when(t == 0)
        def _first_block_picks():
            picks_scr[0] = i_ref[...]
            picks_scr[1] = j_ref[...]
            picks_scr[2] = g_ref[...]

        o_ref[...] = jnp.zeros_like(o_ref)
        gated_scr[...] = jnp.zeros_like(gated_scr)
        sub = lax.broadcasted_iota(jnp.int32, (N_KEYS, LANES), 0).astype(F32)
        zeros = jnp.zeros((N_KEYS, LANES), BF16)

        def onehots(tok):
            irow = picks_scr[0, pl.ds(tok, 1), :]
            jrow = picks_scr[1, pl.ds(tok, 1), :]
            grow = picks_scr[2, pl.ds(tok, 1), :] * 0.5
            pt = jnp.where(sub == irow, 1.0, 0.0).astype(BF16)
            qt = jnp.where(sub == jrow, grow, 0.0).astype(BF16)
            return pt, qt

        def pair_body(p, carry):
            pt_a, qt_a = onehots(p)
            pt_b, qt_b = onehots(p + bt // 2)
            lhs = jnp.concatenate([pt_a, pt_b], axis=1)
            rhs = jnp.concatenate([jnp.concatenate([qt_a, zeros], axis=1),
                                   jnp.concatenate([zeros, qt_b], axis=1)], axis=0)
            w2 = lax.dot_general(lhs, rhs, NT_DIMS, preferred_element_type=F32)
            w_scr[pl.ds(p, N_KEYS, stride=row_stride), :] = w2[:, :LANES]
            w_scr[pl.ds(p + bt // 2, N_KEYS, stride=row_stride), :] = w2[:, LANES:]
            return carry

        lax.fori_loop(0, bt // 2, pair_body, 0, unroll=128)

    def second_matmul():
        o_ref[...] += _dot_by_depth(gated_scr, v_ref)

    @pl.when(e < n_tiles)
    def _sweep_step():
        lane_block = e // steps_per_lane_block
        head0 = (e % steps_per_lane_block) * heads_per_step
        for hh in range(heads_per_step):
            picks = _head_picks(st_ref[2 * hh], st_ref[2 * hh + 1])
            rows = pl.ds(pl.multiple_of((head0 + hh) * PEER_TOPK, PEER_TOPK), PEER_TOPK)
            for c in range(3):
                sel_scr[lane_block, c, rows, :] = picks[c]

        second_matmul()

        act = _dot_by_depth(xn_ref, u_ref, transpose_b=True)
        i0 = e * (tile // N_KEYS)
        w = jnp.concatenate(
            [w_scr[pl.ds(pl.multiple_of((i0 + ii) * row_stride, SUBLANES), bt), :]
             for ii in range(tile // N_KEYS)], axis=1)
        gated = w * (act * (1.0 + lax.erf(act * (2.0 ** -0.5))))
        gated_scr[...] = gated.astype(BF16)

    @pl.when(e == n_tiles)
    def _finish():
        second_matmul()
        y = x1_ref[...] + o_ref[...]
        o_ref[...] = y * lax.rsqrt(jnp.mean(y * y, axis=-1, keepdims=True) + EPS) * fg_ref[...]
        for lb in range(bt // LANES):
            for c in range(3):
                picks_scr[c, lb * LANES:(lb + 1) * LANES, :] = sel_scr[lb, c].T


def _peer(xn, i_first, j_first, g_first, scores_t, u_bf16, v_bf16, x1, fg, bt, tile):
    tokens = xn.shape[0]
    n_blocks = tokens // bt
    n_tiles = u_bf16.shape[0] // tile
    lane_blocks = bt // LANES
    steps_per_lane_block = n_tiles // lane_blocks
    heads_per_step = PEER_HEADS // steps_per_lane_block
    assert (n_tiles == lane_blocks * steps_per_lane_block
            and PEER_HEADS == heads_per_step * steps_per_lane_block)
    row_stride = bt + SUBLANES
    slots = PEER_HEADS * PEER_TOPK
    tok = lambda width: pl.BlockSpec((bt, width), lambda t, e: (t, 0))
    first = pl.BlockSpec((bt, slots), lambda t, e: (0, 0))

    def scores_index(t, e):
        ec = jnp.minimum(e, n_tiles - 1)
        nxt = jnp.minimum(t + 1, n_blocks - 1)
        return (ec % steps_per_lane_block, 0, nxt * lane_blocks + ec // steps_per_lane_block)

    return pl.pallas_call(
        functools.partial(_peer_kernel, row_stride=row_stride, n_tiles=n_tiles,
                          heads_per_step=heads_per_step),
        grid=(n_blocks, n_tiles + 1),
        in_specs=[
            tok(D_MODEL), first, first, first,
            pl.BlockSpec((2 * heads_per_step, N_KEYS, LANES), scores_index),
            pl.BlockSpec((tile, D_MODEL), lambda t, e: (jnp.minimum(e, n_tiles - 1), 0)),
            pl.BlockSpec((tile, D_MODEL), lambda t, e: (jnp.maximum(e - 1, 0), 0)),
            tok(D_MODEL),
            pl.BlockSpec((1, D_MODEL), lambda t, e: (0, 0)),
        ],
        out_specs=tok(D_MODEL),
        out_shape=jax.ShapeDtypeStruct((tokens, D_MODEL), F32),
        scratch_shapes=[pltpu.VMEM((N_KEYS * row_stride, LANES), F32),
                        pltpu.VMEM((bt, tile), BF16),
                        pltpu.VMEM((lane_blocks, 3, slots, LANES), F32),
                        pltpu.VMEM((3, bt, slots), F32)],
        compiler_params=_compiler_params(("arbitrary", "arbitrary"), VMEM_LIMIT_PEER),
    )(xn, i_first, j_first, g_first, scores_t, u_bf16, v_bf16, x1, fg)


def kernel(x, attn_norm_g, w_in, diff_lambda, diff_norm_g, ret_log_decay, ret_norm_g, w_out,
           ffn_norm_g, peer_w_query, peer_sub_keys, peer_u, peer_v, final_norm_g):
    batch, seq, d_model = x.shape
    assert d_model == D_MODEL and seq % LANES == 0
    tokens = batch * seq
    tm = min(ROW_BLOCK, seq)
    x2d = x.reshape(tokens, D_MODEL)

    rot_dim = DIFF_QK_DIM // ROPE_FRACTION
    rope_inv = jnp.power(jnp.float32(ROPE_THETA), -jnp.arange(rot_dim // 2, dtype=F32) * 2.0 / rot_dim)
    ret_inv = 1.0 / jnp.power(jnp.float32(RET_THETA), jnp.linspace(0.0, 1.0, RET_QK_DIM // 2, dtype=F32))
    dtab = _rotary_tables(seq, rope_inv, rot_dim // 2)
    rtab = _rotary_tables(seq, ret_inv, RET_QK_DIM // 2)

    proj = _inproj(x2d, attn_norm_g[0].reshape(1, D_MODEL), w_in[0].astype(BF16), dtab, rtab, seq, tm)
    proj3 = proj.reshape(batch, seq, IN_COLS)

    lam_p = diff_lambda[0].astype(F32)
    lam = (jnp.exp(jnp.sum(lam_p[0] * lam_p[1])) - jnp.exp(jnp.sum(lam_p[2] * lam_p[3]))
           + LAMBDA_INIT).reshape(1).astype(F32)
    attn = _diffattn(lam, proj3, diff_norm_g[0].reshape(1, LANES),
                     min(ATTN_Q_BLOCK, seq))

    log_gamma = -jnp.exp(ret_log_decay[0].astype(F32))
    ret = _retention(log_gamma, proj3, ret_norm_g[0].reshape(1, N_RET_HEADS * LANES),
                     min(RET_CHUNK, seq))

    x1, xn, scores_t = _outproj(
        attn.reshape(tokens, D_MODEL // 2), ret.reshape(tokens, D_MODEL // 2), x2d,
        w_out[0].astype(BF16), ffn_norm_g[0].reshape(1, D_MODEL), peer_w_query[0].astype(BF16),
        peer_sub_keys[0].reshape(2 * PEER_HEADS, N_KEYS, LANES).astype(BF16), tm)

    bt = min(PEER_TOKEN_BLOCK, tokens)
    i_first, j_first, g_first = _topk(scores_t, TOPK_LANE_BLOCK, bt)

    y = _peer(xn, i_first, j_first, g_first, scores_t, peer_u[0].astype(BF16),
              peer_v[0].astype(BF16), x1, final_norm_g.reshape(1, D_MODEL), bt, PEER_EXPERT_TILE)
    return y.reshape(batch, seq, D_MODEL)
```
